```python
import jax, jax.numpy as jnp
from jax import lax
import numpy as np

D_MODEL = 1024
BATCH = 8
SEQ = 2048
DEPTH = 1

MIX_WIDTH = D_MODEL
CONV_WIDTH = MIX_WIDTH // 2
CONV_GROUPS = 8
CONV_K = 3
RET_WIDTH = MIX_WIDTH - CONV_WIDTH
RET_HEADS = 4
RET_HEAD_DIM = RET_WIDTH // RET_HEADS
RET_CHUNK = 128
ROPE_BASE = 10000.0
N_IN_COLS = 3 * CONV_WIDTH + 4 * RET_WIDTH
D_FF = -(-8 * D_MODEL // (3 * 256)) * 256
DN_ALPHA = float((2 * DEPTH) ** 0.25)
DN_BETA = float((8 * DEPTH) ** -0.25)
LN_EPS = 1e-5
N_MOD = 6

kernel_name = "hybrid_shortconv_retention_deepnorm_adaln"


def _layernorm(x, g, b):
    xf = x.astype(jnp.float32)
    mu = jnp.mean(xf, axis=-1, keepdims=True)
    var = jnp.mean(jnp.square(xf - mu), axis=-1, keepdims=True)
    y = (xf - mu) * lax.rsqrt(var + LN_EPS)
    return (y * g.astype(jnp.float32) + b.astype(jnp.float32)).astype(x.dtype)


def _head_groupnorm(y):
    yf = y.astype(jnp.float32)
    mu = jnp.mean(yf, axis=-1, keepdims=True)
    var = jnp.mean(jnp.square(yf - mu), axis=-1, keepdims=True)
    return ((yf - mu) * lax.rsqrt(var + LN_EPS)).astype(y.dtype)


def _rotary(t, seq_len):
    d = t.shape[-1]
    inv_freq = ROPE_BASE ** (-jnp.arange(0, d, 2, dtype=jnp.float32) / d)
    ang = jnp.arange(seq_len, dtype=jnp.float32)[:, None] * inv_freq[None, :]
    cos = jnp.cos(ang)[None, :, None, :].astype(t.dtype)
    sin = jnp.sin(ang)[None, :, None, :].astype(t.dtype)
    t1, t2 = t[..., : d // 2], t[..., d // 2:]
    return jnp.concatenate([t1 * cos - t2 * sin, t1 * sin + t2 * cos], axis=-1)


def _short_conv_mixer(cg, xin, bg, conv_w):
    u = cg * xin
    up = jnp.pad(u, ((0, 0), (CONV_K - 1, 0), (0, 0)))
    s = u.shape[1]
    y = conv_w[0] * up[:, 0:s] + conv_w[1] * up[:, 1:s + 1] + conv_w[2] * up[:, 2:s + 2]
    return bg * y


def _retention(q, k, v):
    bsz, s, h, d = q.shape
    L = RET_CHUNK
    n = s // L
    dt = q.dtype
    log_g = jnp.log1p(-(2.0 ** (-5.0 - jnp.arange(h, dtype=jnp.float32))))
    idx = jnp.arange(L, dtype=jnp.float32)
    diff = idx[:, None] - idx[None, :]
    decay_mask = jnp.where(diff >= 0, jnp.exp(log_g[:, None, None] * jnp.maximum(diff, 0.0)), 0.0).astype(dt)
    zeta = jnp.exp(log_g[:, None] * (L - 1 - idx)[None, :]).astype(dt)
    xi = jnp.exp(log_g[:, None] * (idx + 1)[None, :]).astype(dt)
    chunk_decay = jnp.exp(log_g * L).astype(dt)

    qc = q.reshape(bsz, n, L, h, d)
    kc = k.reshape(bsz, n, L, h, d)
    vc = v.reshape(bsz, n, L, h, d)

    scores = jnp.einsum('bnihd,bnjhd->bhnij', qc, kc) * decay_mask[None, :, None]
    intra = jnp.einsum('bhnij,bnjhe->bnihe', scores, vc)

    chunk_kv = jnp.einsum('bnjhd,hj,bnjhe->nbhde', kc, zeta, vc)

    def step(state, kv):
        return kv + chunk_decay[None, :, None, None] * state, state

    init = jnp.zeros((bsz, h, d, d), dtype=dt)
    _, prev_states = lax.scan(step, init, chunk_kv)
    inter = jnp.einsum('bnihd,hi,nbhde->bnihe', qc, xi, prev_states)
    return (intra + inter).reshape(bsz, s, h, d)


def _hybrid_mixer(h, w_in, conv_w, w_out):
    bsz, s, _ = h.shape
    z = h @ w_in
    wc, wr = CONV_WIDTH, RET_WIDTH
    cg = z[..., 0:wc]
    xin = z[..., wc:2 * wc]
    bg = z[..., 2 * wc:3 * wc]
    o = 3 * wc
    q = z[..., o:o + wr].reshape(bsz, s, RET_HEADS, RET_HEAD_DIM)
    k = z[..., o + wr:o + 2 * wr].reshape(bsz, s, RET_HEADS, RET_HEAD_DIM)
    v = z[..., o + 2 * wr:o + 3 * wr].reshape(bsz, s, RET_HEADS, RET_HEAD_DIM)
    g = z[..., o + 3 * wr:o + 4 * wr]

    y_conv = _short_conv_mixer(cg, xin, bg, conv_w)

    q = _rotary(q, s)
    k = _rotary(k, s) * (RET_HEAD_DIM ** -0.5)
    r = _head_groupnorm(_retention(q, k, v)).reshape(bsz, s, wr)
    y_ret = jax.nn.silu(g) * r

    return jnp.concatenate([y_conv, y_ret], axis=-1) @ w_out


def _swiglu(h, w_gate, w_up, w_down):
    return (jax.nn.silu(h @ w_gate) * (h @ w_up)) @ w_down


def setup_inputs(seed: int = 0) -> dict:
    key = jax.random.key(seed)
    ks = jax.random.split(key, 16)
    d = D_MODEL
    f32 = jnp.float32
    x = jax.random.normal(ks[0], (BATCH, SEQ, d), f32)
    c = jax.random.normal(ks[1], (BATCH, d), f32)
    ada_w = jax.random.normal(ks[2], (DEPTH, d, N_MOD * d), f32) * d ** -0.5
    ada_b = jax.random.normal(ks[3], (DEPTH, N_MOD * d), f32) * 0.01
    w_in = jax.random.normal(ks[4], (DEPTH, d, N_IN_COLS), f32) * d ** -0.5
    v_lo = 3 * CONV_WIDTH + 2 * RET_WIDTH
    w_in = w_in.at[:, :, v_lo:v_lo + RET_WIDTH].multiply(DN_BETA)
    conv_w = jax.random.normal(ks[5], (DEPTH, CONV_K, CONV_WIDTH), f32) * CONV_K ** -0.5
    w_out = jax.random.normal(ks[6], (DEPTH, MIX_WIDTH, d), f32) * MIX_WIDTH ** -0.5 * DN_BETA
    ln1_g = 1.0 + 0.01 * jax.random.normal(ks[7], (DEPTH, d), f32)
    ln1_b = 0.01 * jax.random.normal(ks[8], (DEPTH, d), f32)
    w_gate = jax.random.normal(ks[9], (DEPTH, d, D_FF), f32) * d ** -0.5
    w_up = jax.random.normal(ks[10], (DEPTH, d, D_FF), f32) * d ** -0.5
    w_down = jax.random.normal(ks[11], (DEPTH, D_FF, d), f32) * D_FF ** -0.5 * DN_BETA
    ln2_g = 1.0 + 0.01 * jax.random.normal(ks[12], (DEPTH, d), f32)
    ln2_b = 0.01 * jax.random.normal(ks[13], (DEPTH, d), f32)
    return {"x": x, "c": c, "ada_w": ada_w, "ada_b": ada_b, "w_in": w_in,
            "conv_w": conv_w, "w_out": w_out, "ln1_g": ln1_g, "ln1_b": ln1_b,
            "w_gate": w_gate, "w_up": w_up, "w_down": w_down,
            "ln2_g": ln2_g, "ln2_b": ln2_b}


def reference(x, c, ada_w, ada_b, w_in, conv_w, w_out, ln1_g, ln1_b,
              w_gate, w_up, w_down, ln2_g, ln2_b):
    sc = jax.nn.silu(c)
    for layer in range(DEPTH):
        mod = sc @ ada_w[layer] + ada_b[layer]
        shift_m, scale_m, gate_m, shift_f, scale_f, gate_f = [
            m[:, None, :] for m in jnp.split(mod, N_MOD, axis=-1)]
        h = x * (1.0 + scale_m) + shift_m
        mix = _hybrid_mixer(h, w_in[layer], conv_w[layer], w_out[layer])
        x = _layernorm(DN_ALPHA * x + gate_m * mix, ln1_g[layer], ln1_b[layer])
        h = x * (1.0 + scale_f) + shift_f
        ff = _swiglu(h, w_gate[layer], w_up[layer], w_down[layer])
        x = _layernorm(DN_ALPHA * x + gate_f * ff, ln2_g[layer], ln2_b[layer])
    return x
```

```python
import functools

import jax
import jax.numpy as jnp
from jax import lax
from jax.experimental import pallas as pl
from jax.experimental.pallas import tpu as pltpu

D_MODEL = 1024
CONV_WIDTH = 512
CONV_K = 3
RET_WIDTH = 512
RET_HEADS = 4
RET_HEAD_DIM = 128
RET_CHUNK = 128
ROPE_BASE = 10000.0
N_IN_COLS = 3 * CONV_WIDTH + 4 * RET_WIDTH
D_FF = 2816
DEPTH = 1
DN_ALPHA = float((2 * DEPTH) ** 0.25)
LN_EPS = 1e-5
N_MOD = 6

SEQ_TILE = 512
FFN_TILE = 512
FF_CHUNK = 256
HALO = 8
VMEM_LIMIT_BYTES = 56 * 1024 * 1024

F32 = jnp.float32
BF16 = jnp.bfloat16


def _layernorm(v, g, b):
    mu = jnp.mean(v, axis=-1, keepdims=True)
    d = v - mu
    var = jnp.mean(d * d, axis=-1, keepdims=True)
    return d * lax.rsqrt(var + LN_EPS) * g + b


def _dot(a, b):
    return jnp.dot(a, b, preferred_element_type=F32)


def _adaln_kernel(c_ref, w_ref, b_ref, o_ref):
    sc = jax.nn.silu(c_ref[...])
    o_ref[...] = _dot(sc.astype(BF16), w_ref[...].astype(BF16)) + b_ref[...]


def _adaln(c, ada_w, ada_b):
    bsz, d = c.shape
    n = ada_w.shape[1]
    tn = d
    return pl.pallas_call(
        _adaln_kernel,
        grid=(n // tn,),
        in_specs=[
            pl.BlockSpec((bsz, d), lambda j: (0, 0)),
            pl.BlockSpec((d, tn), lambda j: (0, j)),
            pl.BlockSpec((1, tn), lambda j: (0, j)),
        ],
        out_specs=pl.BlockSpec((bsz, tn), lambda j: (0, j)),
        out_shape=jax.ShapeDtypeStruct((bsz, n), F32),
        compiler_params=pltpu.CompilerParams(dimension_semantics=("arbitrary",)),
        name="adaln_mod",
    )(c, ada_w, ada_b.reshape(1, n))


def _mixer_kernel(x_ref, mod_ref, win_ref, convw_ref, wout_ref, g_ref, b_ref,
                  cosq_ref, sinq_ref, cosk_ref, sink_ref,
                  dmask_ref, xi_ref, zeta_ref, cdecay_ref,
                  o_ref,
                  u_scr, q_scr, k_scr, v_scr, r_scr, y_scr, state_scr):
    ts = x_ref.shape[1]
    wc, wr, hd = CONV_WIDTH, RET_WIDTH, RET_HEAD_DIM
    j = pl.program_id(1)

    @pl.when(j == 0)
    def _():
        state_scr[...] = jnp.zeros_like(state_scr)
        u_scr[0:HALO, :] = jnp.zeros((HALO, wc), F32)

    x = x_ref[0]
    shift, scale, gate = mod_ref[0, 0:1, :], mod_ref[0, 1:2, :], mod_ref[0, 2:3, :]
    h = (x * (1.0 + scale) + shift).astype(BF16)

    cg = _dot(h, win_ref[:, 0:wc])
    xin = _dot(h, win_ref[:, wc:2 * wc])
    u_scr[HALO:HALO + ts, :] = cg * xin
    bg = _dot(h, win_ref[:, 2 * wc:3 * wc])
    conv = (convw_ref[0:1, :] * u_scr[HALO - 2:HALO - 2 + ts, :]
            + convw_ref[1:2, :] * u_scr[HALO - 1:HALO - 1 + ts, :]
            + convw_ref[2:3, :] * u_scr[HALO:HALO + ts, :])
    y_scr[:, 0:wc] = (bg * conv).astype(BF16)
    u_scr[0:HALO, :] = u_scr[ts:ts + HALO, :]

    o = 3 * wc
    q_scr[...] = _dot(h, win_ref[:, o:o + wr])
    k_scr[...] = _dot(h, win_ref[:, o + wr:o + 2 * wr])
    v_scr[...] = _dot(h, win_ref[:, o + 2 * wr:o + 3 * wr])

    for c in range(ts // RET_CHUNK):
        rows = slice(c * RET_CHUNK, (c + 1) * RET_CHUNK)
        cq, sq = cosq_ref[rows, :], sinq_ref[rows, :]
        ck, sk = cosk_ref[rows, :], sink_ref[rows, :]
        for hh in range(RET_HEADS):
            cols = slice(hh * hd, (hh + 1) * hd)
            qf = q_scr[rows, cols]
            kf = k_scr[rows, cols]
            vb = v_scr[rows, cols].astype(BF16)
            qf = qf * cq + pltpu.roll(qf, hd // 2, 1) * sq
            kf = kf * ck + pltpu.roll(kf, hd // 2, 1) * sk
            s = lax.dot_general(qf.astype(BF16), kf.astype(BF16),
                                (((1,), (1,)), ((), ())), preferred_element_type=F32)
            p = (s * dmask_ref[hh]).astype(BF16)
            qx = (qf * xi_ref[hh]).astype(BF16)
            st = state_scr[hh]
            out = _dot(jnp.concatenate([p, qx], axis=1),
                       jnp.concatenate([vb, st.astype(BF16)], axis=0))
            kz = (kf * zeta_ref[hh]).astype(BF16)
            state_scr[hh] = cdecay_ref[hh] * st + lax.dot_general(
                kz, vb, (((0,), (0,)), ((), ())), preferred_element_type=F32)
            mu = jnp.mean(out, axis=-1, keepdims=True)
            dlt = out - mu
            var = jnp.mean(dlt * dlt, axis=-1, keepdims=True)
            r_scr[rows, cols] = dlt * lax.rsqrt(var + LN_EPS)

    gz = _dot(h, win_ref[:, o + 3 * wr:o + 4 * wr])
    y_scr[:, wc:wc + wr] = (jax.nn.silu(gz) * r_scr[...]).astype(BF16)

    mix = _dot(y_scr[...], wout_ref[...])
    o_ref[0] = _layernorm(DN_ALPHA * x + gate * mix, g_ref[...], b_ref[...])


def _const_spec(shape, ngrid):
    zeros = (0,) * len(shape)
    if ngrid == 1:
        return pl.BlockSpec(shape, lambda i: zeros, pipeline_mode=pl.Buffered(1))
    return pl.BlockSpec(shape, lambda b, j: zeros, pipeline_mode=pl.Buffered(1))


def _mixer(x, mod3, w_in, conv_w, w_out, ln_g, ln_b, tables, consts):
    bsz, seq, d = x.shape
    ts = SEQ_TILE
    cosq, sinq, cosk, sink = tables
    dmask, xi_b, zeta_b, cdecay = consts
    hd = RET_HEAD_DIM
    tab_spec = pl.BlockSpec((ts, hd), lambda b, j: (j, 0))
    return pl.pallas_call(
        _mixer_kernel,
        grid=(bsz, seq // ts),
        in_specs=[
            pl.BlockSpec((1, ts, d), lambda b, j: (b, j, 0)),
            pl.BlockSpec((1, N_MOD, d), lambda b, j: (b, 0, 0)),
            _const_spec((d, N_IN_COLS), 2),
            _const_spec((CONV_K, CONV_WIDTH), 2),
            _const_spec((d, d), 2),
            _const_spec((1, d), 2),
            _const_spec((1, d), 2),
            tab_spec, tab_spec, tab_spec, tab_spec,
            _const_spec((RET_HEADS, RET_CHUNK, RET_CHUNK), 2),
            _const_spec((RET_HEADS, RET_CHUNK, hd), 2),
            _const_spec((RET_HEADS, RET_CHUNK, hd), 2),
            pl.BlockSpec(memory_space=pltpu.SMEM),
        ],
        out_specs=pl.BlockSpec((1, ts, d), lambda b, j: (b, j, 0)),
        out_shape=jax.ShapeDtypeStruct((bsz, seq, d), F32),
        scratch_shapes=[
            pltpu.VMEM((ts + HALO, CONV_WIDTH), F32),
            pltpu.VMEM((ts, RET_WIDTH), F32),
            pltpu.VMEM((ts, RET_WIDTH), F32),
            pltpu.VMEM((ts, RET_WIDTH), F32),
            pltpu.VMEM((ts, RET_WIDTH), F32),
            pltpu.VMEM((ts, d), BF16),
            pltpu.VMEM((RET_HEADS, hd, hd), F32),
        ],
        compiler_params=pltpu.CompilerParams(
            dimension_semantics=("arbitrary", "arbitrary"),
            vmem_limit_bytes=VMEM_LIMIT_BYTES),
        name="mixer_ln1",
    )(x, mod3, w_in, conv_w, w_out, ln_g, ln_b, cosq, sinq, cosk, sink,
      dmask, xi_b, zeta_b, cdecay)


def _ffn_kernel(x_ref, mod_ref, wgu_ref, wd_ref, g_ref, b_ref, o_ref, act_scr):
    x = x_ref[...]
    shift, scale, gate = mod_ref[0, 3:4, :], mod_ref[0, 4:5, :], mod_ref[0, 5:6, :]
    h = (x * (1.0 + scale) + shift).astype(BF16)
    fc = FF_CHUNK
    for c in range(D_FF // fc):
        gu = _dot(h, wgu_ref[:, 2 * c * fc:2 * (c + 1) * fc])
        act_scr[:, c * fc:(c + 1) * fc] = (jax.nn.silu(gu[:, 0:fc]) * gu[:, fc:2 * fc]).astype(BF16)
    ff = _dot(act_scr[...], wd_ref[...])
    o_ref[...] = _layernorm(DN_ALPHA * x + gate * ff, g_ref[...], b_ref[...])


def _ffn(x1, mod3, w_gu, w_down, ln_g, ln_b, seq):
    t, d = x1.shape
    tm = FFN_TILE
    tiles_per_seq = seq // tm
    return pl.pallas_call(
        _ffn_kernel,
        grid=(t // tm,),
        in_specs=[
            pl.BlockSpec((tm, d), lambda i: (i, 0)),
            pl.BlockSpec((1, N_MOD, d), lambda i: (i // tiles_per_seq, 0, 0)),
            _const_spec((d, 2 * D_FF), 1),
            _const_spec((D_FF, d), 1),
            _const_spec((1, d), 1),
            _const_spec((1, d), 1),
        ],
        out_specs=pl.BlockSpec((tm, d), lambda i: (i, 0)),
        out_shape=jax.ShapeDtypeStruct((t, d), F32),
        scratch_shapes=[pltpu.VMEM((tm, D_FF), BF16)],
        compiler_params=pltpu.CompilerParams(
            dimension_semantics=("arbitrary",),
            vmem_limit_bytes=VMEM_LIMIT_BYTES),
        name="ffn_ln2",
    )(x1, mod3, w_gu, w_down, ln_g, ln_b)


def _rotary_tables(seq):
    d = RET_HEAD_DIM
    inv_freq = ROPE_BASE ** (-jnp.arange(0, d, 2, dtype=F32) / d)
    ang = jnp.arange(seq, dtype=F32)[:, None] * inv_freq[None, :]
    cos, sin = jnp.cos(ang), jnp.sin(ang)
    cosq = jnp.concatenate([cos, cos], axis=-1)
    sinq = jnp.concatenate([-sin, sin], axis=-1)
    kscale = d ** -0.5
    return cosq, sinq, cosq * kscale, sinq * kscale


def _retention_consts():
    h, L, d = RET_HEADS, RET_CHUNK, RET_HEAD_DIM
    log_g = jnp.log1p(-(2.0 ** (-5.0 - jnp.arange(h, dtype=F32))))
    idx = jnp.arange(L, dtype=F32)
    diff = idx[:, None] - idx[None, :]
    dmask = jnp.where(diff >= 0, jnp.exp(log_g[:, None, None] * jnp.maximum(diff, 0.0)), 0.0)
    zeta = jnp.exp(log_g[:, None] * (L - 1 - idx)[None, :])
    xi = jnp.exp(log_g[:, None] * (idx + 1)[None, :])
    cdecay = jnp.exp(log_g * L)
    bcast = lambda a: jnp.broadcast_to(a[:, :, None], (h, L, d))
    return dmask.astype(F32), bcast(xi).astype(F32), bcast(zeta).astype(F32), cdecay.astype(F32)


def kernel(x, c, ada_w, ada_b, w_in, conv_w, w_out, ln1_g, ln1_b, w_gate, w_up, w_down, ln2_g, ln2_b):
    bsz, seq, d = x.shape
    assert ada_w.shape[0] == DEPTH
    assert seq % SEQ_TILE == 0 and seq % FFN_TILE == 0 and D_FF % FF_CHUNK == 0
    tables = _rotary_tables(seq)
    consts = _retention_consts()
    for layer in range(DEPTH):
        mod3 = _adaln(c, ada_w[layer], ada_b[layer]).reshape(bsz, N_MOD, d)
        nc = D_FF // FF_CHUNK
        w_gu = jnp.concatenate(
            [w_gate[layer].astype(BF16).reshape(d, nc, FF_CHUNK),
             w_up[layer].astype(BF16).reshape(d, nc, FF_CHUNK)], axis=-1).reshape(d, 2 * D_FF)
        x1 = _mixer(x, mod3, w_in[layer].astype(BF16), conv_w[layer], w_out[layer].astype(BF16),
                    ln1_g[layer].reshape(1, d), ln1_b[layer].reshape(1, d), tables, consts)
        x2 = _ffn(x1.reshape(bsz * seq, d), mod3, w_gu, w_down[layer].astype(BF16),
                  ln2_g[layer].reshape(1, d), ln2_b[layer].reshape(1, d), seq)
        x = x2.reshape(bsz, seq, d)
    return x
```

```python
import functools

import jax
import jax.numpy as jnp
import numpy as np
from jax import lax
from jax.experimental import pallas as pl
from jax.experimental.pallas import tpu as pltpu

D_MODEL = 1024
CONV_WIDTH = 512
CONV_K = 3
RET_WIDTH = 512
RET_HEADS = 4
RET_HEAD_DIM = 128
RET_CHUNK = 128
ROPE_BASE = 10000.0
N_IN_COLS = 3 * CONV_WIDTH + 4 * RET_WIDTH
D_FF = 2816
DEPTH = 1
DN_ALPHA = float((2 * DEPTH) ** 0.25)
LN_EPS = 1e-5
N_MOD = 6

SEQ_TILE = 512
FFN_TILE = 512
SUB_ROWS = 256
FF_CHUNK = 256
HALO = 8
VMEM_LIMIT_BYTES = 56 * 1024 * 1024

F32 = jnp.float32
BF16 = jnp.bfloat16


def _layernorm(v, g, b):
    mu = jnp.mean(v, axis=-1, keepdims=True)
    d = v - mu
    var = jnp.mean(d * d, axis=-1, keepdims=True)
    return d * lax.rsqrt(var + LN_EPS) * g + b


def _dot(a, b):
    return jnp.dot(a, b, preferred_element_type=F32)


def _adaln_kernel(c_ref, w_ref, b_ref, o_ref):
    sc = jax.nn.silu(c_ref[...])
    o_ref[0] = _dot(sc.astype(BF16), w_ref[...].astype(BF16)) + b_ref[...]


def _adaln(c, ada_w, ada_b):
    bsz, d = c.shape
    return pl.pallas_call(
        _adaln_kernel,
        grid=(N_MOD,),
        in_specs=[
            pl.BlockSpec((bsz, d), lambda j: (0, 0)),
            pl.BlockSpec((d, d), lambda j: (0, j)),
            pl.BlockSpec((1, d), lambda j: (0, j)),
        ],
        out_specs=pl.BlockSpec((1, bsz, d), lambda j: (j, 0, 0)),
        out_shape=jax.ShapeDtypeStruct((N_MOD, bsz, d), F32),
        compiler_params=pltpu.CompilerParams(dimension_semantics=("arbitrary",)),
        name="adaln_mod",
    )(c, ada_w, ada_b.reshape(1, N_MOD * d))


def _mixer_kernel(x_ref, mod_ref, win_ref, convw_ref, wout_ref, g_ref, b_ref,
                  cosq_ref, sinq_ref, cosk_ref, sink_ref,
                  dmask_ref, xi_ref, zeta_ref, cdecay_ref,
                  o_ref, u_scr, state_scr):
    ts = x_ref.shape[1]
    wc, wr, hd, L = CONV_WIDTH, RET_WIDTH, RET_HEAD_DIM, RET_CHUNK
    b = pl.program_id(0)
    j = pl.program_id(1)

    @pl.when(j == 0)
    def _():
        state_scr[...] = jnp.zeros_like(state_scr)
        u_scr[0:HALO, :] = jnp.zeros((HALO, wc), F32)

    shift = mod_ref[0, pl.ds(b, 1), :]
    scale = mod_ref[1, pl.ds(b, 1), :]
    gate = mod_ref[2, pl.ds(b, 1), :]
    states = [state_scr[hh] for hh in range(RET_HEADS)]
    o = 3 * wc

    for r0 in range(0, ts, SUB_ROWS):
        x = x_ref[0, r0:r0 + SUB_ROWS, :]
        h = (x * (1.0 + scale) + shift).astype(BF16)

        cg = _dot(h, win_ref[:, 0:wc])
        xin = _dot(h, win_ref[:, wc:2 * wc])
        u0 = HALO + r0
        u_scr[u0:u0 + SUB_ROWS, :] = cg * xin
        bg = _dot(h, win_ref[:, 2 * wc:3 * wc])
        conv = (convw_ref[0:1, :] * u_scr[u0 - 2:u0 - 2 + SUB_ROWS, :]
                + convw_ref[1:2, :] * u_scr[u0 - 1:u0 - 1 + SUB_ROWS, :]
                + convw_ref[2:3, :] * u_scr[u0:u0 + SUB_ROWS, :])
        y_conv = (bg * conv).astype(BF16)

        q = _dot(h, win_ref[:, o:o + wr])
        k = _dot(h, win_ref[:, o + wr:o + 2 * wr])
        v = _dot(h, win_ref[:, o + 2 * wr:o + 3 * wr])
        unit = {}
        for c in range(SUB_ROWS // L):
            rows = slice(c * L, (c + 1) * L)
            trow = slice(r0 + c * L, r0 + (c + 1) * L)
            cq, sq = cosq_ref[trow, :], sinq_ref[trow, :]
            ck, sk = cosk_ref[trow, :], sink_ref[trow, :]
            for hh in range(RET_HEADS):
                cols = slice(hh * hd, (hh + 1) * hd)
                qf, kf = q[rows, cols], k[rows, cols]
                vb = v[rows, cols].astype(BF16)
                qf = qf * cq + pltpu.roll(qf, hd // 2, 1) * sq
                kf = kf * ck + pltpu.roll(kf, hd // 2, 1) * sk
                s = lax.dot_general(qf.astype(BF16), kf.astype(BF16),
                                    (((1,), (1,)), ((), ())), preferred_element_type=F32)
                kv = lax.dot_general((kf * zeta_ref[hh]).astype(BF16), vb,
                                     (((0,), (0,)), ((), ())), preferred_element_type=F32)
                p = (s * dmask_ref[hh]).astype(BF16)
                qx = (qf * xi_ref[hh]).astype(BF16)
                unit[c, hh] = (jnp.concatenate([p, qx], axis=1), vb, kv)

        r_rows = []
        for c in range(SUB_ROWS // L):
            r_heads = []
            for hh in range(RET_HEADS):
                pq, vb, kv = unit[c, hh]
                out = _dot(pq, jnp.concatenate([vb, states[hh].astype(BF16)], axis=0))
                states[hh] = cdecay_ref[hh] * states[hh] + kv
                mu = jnp.mean(out, axis=-1, keepdims=True)
                dlt = out - mu
                var = jnp.mean(dlt * dlt, axis=-1, keepdims=True)
                r_heads.append(dlt * lax.rsqrt(var + LN_EPS))
            r_rows.append(jnp.concatenate(r_heads, axis=1))
        r = jnp.concatenate(r_rows, axis=0)

        gz = _dot(h, win_ref[:, o + 3 * wr:o + 4 * wr])
        y_ret = (jax.nn.silu(gz) * r).astype(BF16)

        mix = _dot(jnp.concatenate([y_conv, y_ret], axis=1), wout_ref[...])
        o_ref[0, r0:r0 + SUB_ROWS, :] = _layernorm(DN_ALPHA * x + gate * mix, g_ref[...], b_ref[...])

    for hh in range(RET_HEADS):
        state_scr[hh] = states[hh]
    u_scr[0:HALO, :] = u_scr[ts:ts + HALO, :]


def _const_spec(shape, ngrid):
    zeros = (0,) * len(shape)
    if ngrid == 1:
        return pl.BlockSpec(shape, lambda i: zeros, pipeline_mode=pl.Buffered(1))
    return pl.BlockSpec(shape, lambda b, j: zeros, pipeline_mode=pl.Buffered(1))


def _mixer(x, mod, w_in, conv_w, w_out, ln_g, ln_b, tables, consts):
    bsz, seq, d = x.shape
    ts = SEQ_TILE
    cosq, sinq, cosk, sink = tables
    dmask, xi_b, zeta_b, cdecay = consts
    hd = RET_HEAD_DIM
    tab_spec = pl.BlockSpec((ts, hd), lambda b, j: (j, 0))
    return pl.pallas_call(
        _mixer_kernel,
        grid=(bsz, seq // ts),
        in_specs=[
            pl.BlockSpec((1, ts, d), lambda b, j: (b, j, 0)),
            _const_spec((N_MOD, bsz, d), 2),
            _const_spec((d, N_IN_COLS), 2),
            _const_spec((CONV_K, CONV_WIDTH), 2),
            _const_spec((d, d), 2),
            _const_spec((1, d), 2),
            _const_spec((1, d), 2),
            tab_spec, tab_spec, tab_spec, tab_spec,
            _const_spec((RET_HEADS, RET_CHUNK, RET_CHUNK), 2),
            _const_spec((RET_HEADS, RET_CHUNK, hd), 2),
            _const_spec((RET_HEADS, RET_CHUNK, hd), 2),
            pl.BlockSpec(memory_space=pltpu.SMEM),
        ],
        out_specs=pl.BlockSpec((1, ts, d), lambda b, j: (b, j, 0)),
        out_shape=jax.ShapeDtypeStruct((bsz, seq, d), F32),
        scratch_shapes=[
            pltpu.VMEM((ts + HALO, CONV_WIDTH), F32),
            pltpu.VMEM((RET_HEADS, hd, hd), F32),
        ],
        compiler_params=pltpu.CompilerParams(
            dimension_semantics=("arbitrary", "arbitrary"),
            vmem_limit_bytes=VMEM_LIMIT_BYTES),
        name="mixer_ln1",
    )(x, mod, w_in, conv_w, w_out, ln_g, ln_b, cosq, sinq, cosk, sink,
      dmask, xi_b, zeta_b, cdecay)


def _ffn_kernel(tiles_per_seq, x_ref, mod_ref, wg_ref, wu_ref, wd_ref, g_ref, b_ref, o_ref, act_scr):
    tm = x_ref.shape[0]
    b = pl.program_id(0) // tiles_per_seq
    shift = mod_ref[3, pl.ds(b, 1), :]
    scale = mod_ref[4, pl.ds(b, 1), :]
    gate = mod_ref[5, pl.ds(b, 1), :]
    fc = FF_CHUNK
    for r0 in range(0, tm, SUB_ROWS):
        rows = slice(r0, r0 + SUB_ROWS)
        x = x_ref[rows, :]
        h = (x * (1.0 + scale) + shift).astype(BF16)
        for c in range(D_FF // fc):
            cols = slice(c * fc, (c + 1) * fc)
            act_scr[rows, cols] = (jax.nn.silu(_dot(h, wg_ref[:, cols])) * _dot(h, wu_ref[:, cols])).astype(BF16)
        ff = _dot(act_scr[rows, :], wd_ref[...])
        o_ref[rows, :] = _layernorm(DN_ALPHA * x + gate * ff, g_ref[...], b_ref[...])


def _ffn(x1, mod, w_gate, w_up, w_down, ln_g, ln_b, seq):
    t, d = x1.shape
    bsz = t // seq
    tm = FFN_TILE
    return pl.pallas_call(
        functools.partial(_ffn_kernel, seq // tm),
        grid=(t // tm,),
        in_specs=[
            pl.BlockSpec((tm, d), lambda i: (i, 0)),
            _const_spec((N_MOD, bsz, d), 1),
            _const_spec((d, D_FF), 1),
            _const_spec((d, D_FF), 1),
            _const_spec((D_FF, d), 1),
            _const_spec((1, d), 1),
            _const_spec((1, d), 1),
        ],
        out_specs=pl.BlockSpec((tm, d), lambda i: (i, 0)),
        out_shape=jax.ShapeDtypeStruct((t, d), F32),
        scratch_shapes=[pltpu.VMEM((tm, D_FF), BF16)],
        compiler_params=pltpu.CompilerParams(
            dimension_semantics=("arbitrary",),
            vmem_limit_bytes=VMEM_LIMIT_BYTES),
        name="ffn_ln2",
    )(x1, mod, w_gate, w_up, w_down, ln_g, ln_b)


def _rotary_tables(seq):
    d = RET_HEAD_DIM
    inv_freq = ROPE_BASE ** (-np.arange(0, d, 2, dtype=np.float64) / d)
    ang = np.arange(seq, dtype=np.float64)[:, None] * inv_freq[None, :]
    cos, sin = np.cos(ang), np.sin(ang)
    cosq = np.concatenate([cos, cos], axis=-1)
    sinq = np.concatenate([-sin, sin], axis=-1)
    kscale = d ** -0.5
    return tuple(jnp.asarray(t.astype(np.float32)) for t in (cosq, sinq, cosq * kscale, sinq * kscale))


def _retention_consts():
    h, L, d = RET_HEADS, RET_CHUNK, RET_HEAD_DIM
    log_g = np.log1p(-(2.0 ** (-5.0 - np.arange(h, dtype=np.float64))))
    idx = np.arange(L, dtype=np.float64)
    diff = idx[:, None] - idx[None, :]
    dmask = np.where(diff >= 0, np.exp(log_g[:, None, None] * np.maximum(diff, 0.0)), 0.0)
    zeta = np.exp(log_g[:, None] * (L - 1 - idx)[None, :])
    xi = np.exp(log_g[:, None] * (idx + 1)[None, :])
    cdecay = np.exp(log_g * L)
    bcast = lambda a: np.broadcast_to(a[:, :, None], (h, L, d))
    return tuple(jnp.asarray(np.ascontiguousarray(t).astype(np.float32))
                 for t in (dmask, bcast(xi), bcast(zeta), cdecay))


def kernel(x, c, ada_w, ada_b, w_in, conv_w, w_out, ln1_g, ln1_b, w_gate, w_up, w_down, ln2_g, ln2_b):
    bsz, seq, d = x.shape
    assert ada_w.shape[0] == DEPTH
    assert seq % SEQ_TILE == 0 and seq % FFN_TILE == 0 and D_FF % FF_CHUNK == 0
    assert SEQ_TILE % SUB_ROWS == 0 and FFN_TILE % SUB_ROWS == 0 and SUB_ROWS % RET_CHUNK == 0
    tables = _rotary_tables(seq)
    consts = _retention_consts()
    for layer in range(DEPTH):
        mod = _adaln(c, ada_w[layer], ada_b[layer])
        x1 = _mixer(x, mod, w_in[layer].astype(BF16), conv_w[layer], w_out[layer].astype(BF16),
                    ln1_g[layer].reshape(1, d), ln1_b[layer].reshape(1, d), tables, consts)
        x2 = _ffn(x1.reshape(bsz * seq, d), mod, w_gate[layer].astype(BF16), w_up[layer].astype(BF16),
                  w_down[layer].astype(BF16), ln2_g[layer].reshape(1, d), ln2_b[layer].reshape(1, d), seq)
        x = x2.reshape(bsz, seq, d)
    return x
```

```python
import functools

import jax
import jax.numpy as jnp
import numpy as np
from jax import lax
from jax.experimental import pallas as pl
from jax.experimental.pallas import tpu as pltpu

D_MODEL = 1024
CONV_WIDTH = 512
CONV_K = 3
RET_WIDTH = 512
RET_HEADS = 4
RET_HEAD_DIM = 128
RET_CHUNK = 128
ROPE_BASE = 10000.0
N_IN_COLS = 3 * CONV_WIDTH + 4 * RET_WIDTH
D_FF = 2816
DEPTH = 1
DN_ALPHA = float((2 * DEPTH) ** 0.25)
LN_EPS = 1e-5
N_MOD = 6

SEQ_TILE = 1024
FFN_TILE = 1024
SUB_ROWS = 256
FFN_SUB_ROWS = 256
FF_CHUNK = 256
HALO = 8
VMEM_LIMIT_BYTES = 56 * 1024 * 1024

F32 = jnp.float32
BF16 = jnp.bfloat16


def _layernorm(v, g, b):
    mu = jnp.mean(v, axis=-1, keepdims=True)
    d = v - mu
    var = jnp.mean(d * d, axis=-1, keepdims=True)
    return d * lax.rsqrt(var + LN_EPS) * g + b


def _dot(a, b):
    return jnp.dot(a, b, preferred_element_type=F32)


def _adaln_kernel(c_ref, w_ref, b_ref, o_ref):
    sc = jax.nn.silu(c_ref[...])
    o_ref[0] = _dot(sc.astype(BF16), w_ref[...].astype(BF16)) + b_ref[...]


def _adaln(c, ada_w, ada_b):
    bsz, d = c.shape
    return pl.pallas_call(
        _adaln_kernel,
        grid=(N_MOD,),
        in_specs=[
            pl.BlockSpec((bsz, d), lambda j: (0, 0)),
            pl.BlockSpec((d, d), lambda j: (0, j)),
            pl.BlockSpec((1, d), lambda j: (0, j)),
        ],
        out_specs=pl.BlockSpec((1, bsz, d), lambda j: (j, 0, 0)),
        out_shape=jax.ShapeDtypeStruct((N_MOD, bsz, d), F32),
        compiler_params=pltpu.CompilerParams(dimension_semantics=("arbitrary",)),
        name="adaln_mod",
    )(c, ada_w, ada_b.reshape(1, N_MOD * d))


def _mixer_kernel(x_ref, mod_ref, win_ref, convw_ref, wout_ref, g_ref, b_ref,
                  cosq_ref, sinq_ref, cosk_ref, sink_ref,
                  dmask_ref, xi_ref, zeta_ref, cdecay_ref,
                  o_ref, u_scr, state_scr):
    ts = x_ref.shape[1]
    wc, wr, hd, L = CONV_WIDTH, RET_WIDTH, RET_HEAD_DIM, RET_CHUNK
    b = pl.program_id(0)
    j = pl.program_id(1)

    @pl.when(j == 0)
    def _():
        state_scr[...] = jnp.zeros_like(state_scr)
        u_scr[0:HALO, :] = jnp.zeros((HALO, wc), F32)

    shift = mod_ref[0, pl.ds(b, 1), :]
    scale = mod_ref[1, pl.ds(b, 1), :]
    gate = mod_ref[2, pl.ds(b, 1), :]
    states = [state_scr[hh] for hh in range(RET_HEADS)]
    o = 3 * wc

    for r0 in range(0, ts, SUB_ROWS):
        x = x_ref[0, r0:r0 + SUB_ROWS, :]
        h = (x * (1.0 + scale) + shift).astype(BF16)

        cg = _dot(h, win_ref[:, 0:wc])
        xin = _dot(h, win_ref[:, wc:2 * wc])
        u0 = HALO + r0
        u_scr[u0:u0 + SUB_ROWS, :] = cg * xin
        bg = _dot(h, win_ref[:, 2 * wc:3 * wc])
        conv = (convw_ref[0:1, :] * u_scr[u0 - 2:u0 - 2 + SUB_ROWS, :]
                + convw_ref[1:2, :] * u_scr[u0 - 1:u0 - 1 + SUB_ROWS, :]
                + convw_ref[2:3, :] * u_scr[u0:u0 + SUB_ROWS, :])
        y_conv = (bg * conv).astype(BF16)

        q = _dot(h, win_ref[:, o:o + wr])
        k = _dot(h, win_ref[:, o + wr:o + 2 * wr])
        v = _dot(h, win_ref[:, o + 2 * wr:o + 3 * wr])
        unit = {}
        for c in range(SUB_ROWS // L):
            rows = slice(c * L, (c + 1) * L)
            trow = slice(r0 + c * L, r0 + (c + 1) * L)
            cq, sq = cosq_ref[trow, :], sinq_ref[trow, :]
            ck, sk = cosk_ref[trow, :], sink_ref[trow, :]
            for hh in range(RET_HEADS):
                cols = slice(hh * hd, (hh + 1) * hd)
                qf, kf = q[rows, cols], k[rows, cols]
                vb = v[rows, cols].astype(BF16)
                qf = qf * cq + pltpu.roll(qf, hd // 2, 1) * sq
                kf = kf * ck + pltpu.roll(kf, hd // 2, 1) * sk
                s = lax.dot_general(qf.astype(BF16), kf.astype(BF16),
                                    (((1,), (1,)), ((), ())), preferred_element_type=F32)
                kv = lax.dot_general((kf * zeta_ref[hh]).astype(BF16), vb,
                                     (((0,), (0,)), ((), ())), preferred_element_type=F32)
                p = (s * dmask_ref[hh]).astype(BF16)
                qx = (qf * xi_ref[hh]).astype(BF16)
                unit[c, hh] = (jnp.concatenate([p, qx], axis=1), vb, kv)

        r_rows = []
        for c in range(SUB_ROWS // L):
            r_heads = []
            for hh in range(RET_HEADS):
                pq, vb, kv = unit[c, hh]
                out = _dot(pq, jnp.concatenate([vb, states[hh].astype(BF16)], axis=0))
                states[hh] = cdecay_ref[hh] * states[hh] + kv
                mu = jnp.mean(out, axis=-1, keepdims=True)
                dlt = out - mu
                var = jnp.mean(dlt * dlt, axis=-1, keepdims=True)
                r_heads.append(dlt * lax.rsqrt(var + LN_EPS))
            r_rows.append(jnp.concatenate(r_heads, axis=1))
        r = jnp.concatenate(r_rows, axis=0)

        gz = _dot(h, win_ref[:, o + 3 * wr:o + 4 * wr])
        y_ret = (jax.nn.silu(gz) * r).astype(BF16)

        mix = _dot(jnp.concatenate([y_conv, y_ret], axis=1), wout_ref[...])
        o_ref[0, r0:r0 + SUB_ROWS, :] = _layernorm(DN_ALPHA * x + gate * mix, g_ref[...], b_ref[...])

    for hh in range(RET_HEADS):
        state_scr[hh] = states[hh]
    u_scr[0:HALO, :] = u_scr[ts:ts + HALO, :]


def _const_spec(shape, ngrid):
    zeros = (0,) * len(shape)
    if ngrid == 1:
        return pl.BlockSpec(shape, lambda i: zeros, pipeline_mode=pl.Buffered(1))
    return pl.BlockSpec(shape, lambda b, j: zeros, pipeline_mode=pl.Buffered(1))


def _mixer(x, mod, w_in, conv_w, w_out, ln_g, ln_b, tables, consts):
    bsz, seq, d = x.shape
    ts = SEQ_TILE
    cosq, sinq, cosk, sink = tables
    dmask, xi_b, zeta_b, cdecay = consts
    hd = RET_HEAD_DIM
    tab_spec = pl.BlockSpec((ts, hd), lambda b, j: (j, 0))
    return pl.pallas_call(
        _mixer_kernel,
        grid=(bsz, seq // ts),
        in_specs=[
            pl.BlockSpec((1, ts, d), lambda b, j: (b, j, 0)),
            _const_spec((N_MOD, bsz, d), 2),
            _const_spec((d, N_IN_COLS), 2),
            _const_spec((CONV_K, CONV_WIDTH), 2),
            _const_spec((d, d), 2),
            _const_spec((1, d), 2),
            _const_spec((1, d), 2),
            tab_spec, tab_spec, tab_spec, tab_spec,
            _const_spec((RET_HEADS, RET_CHUNK, RET_CHUNK), 2),
            _const_spec((RET_HEADS, RET_CHUNK, hd), 2),
            _const_spec((RET_HEADS, RET_CHUNK, hd), 2),
            pl.BlockSpec(memory_space=pltpu.SMEM),
        ],
        out_specs=pl.BlockSpec((1, ts, d), lambda b, j: (b, j, 0)),
        out_shape=jax.ShapeDtypeStruct((bsz, seq, d), F32),
        scratch_shapes=[
            pltpu.VMEM((ts + HALO, CONV_WIDTH), F32),
            pltpu.VMEM((RET_HEADS, hd, hd), F32),
        ],
        compiler_params=pltpu.CompilerParams(
            dimension_semantics=("arbitrary", "arbitrary"),
            vmem_limit_bytes=VMEM_LIMIT_BYTES),
        name="mixer_ln1",
    )(x, mod, w_in, conv_w, w_out, ln_g, ln_b, cosq, sinq, cosk, sink,
      dmask, xi_b, zeta_b, cdecay)


def _ffn_kernel(tiles_per_seq, x_ref, mod_ref, wg_ref, wu_ref, wd_ref, g_ref, b_ref, o_ref, act_scr):
    tm = x_ref.shape[0]
    b = pl.program_id(0) // tiles_per_seq
    shift = mod_ref[3, pl.ds(b, 1), :]
    scale = mod_ref[4, pl.ds(b, 1), :]
    gate = mod_ref[5, pl.ds(b, 1), :]
    fc = FF_CHUNK
    for r0 in range(0, tm, FFN_SUB_ROWS):
        rows = slice(r0, r0 + FFN_SUB_ROWS)
        x = x_ref[rows, :]
        h = (x * (1.0 + scale) + shift).astype(BF16)
        for c in range(D_FF // fc):
            cols = slice(c * fc, (c + 1) * fc)
            act_scr[rows, cols] = (jax.nn.silu(_dot(h, wg_ref[:, cols])) * _dot(h, wu_ref[:, cols])).astype(BF16)
        ff = _dot(act_scr[rows, :], wd_ref[...])
        o_ref[rows, :] = _layernorm(DN_ALPHA * x + gate * ff, g_ref[...], b_ref[...])


def _ffn(x1, mod, w_gate, w_up, w_down, ln_g, ln_b, seq):
    t, d = x1.shape
    bsz = t // seq
    tm = FFN_TILE
    return pl.pallas_call(
        functools.partial(_ffn_kernel, seq // tm),
        grid=(t // tm,),
        in_specs=[
            pl.BlockSpec((tm, d), lambda i: (i, 0)),
            _const_spec((N_MOD, bsz, d), 1),
            _const_spec((d, D_FF), 1),
            _const_spec((d, D_FF), 1),
            _const_spec((D_FF, d), 1),
            _const_spec((1, d), 1),
            _const_spec((1, d), 1),
        ],
        out_specs=pl.BlockSpec((tm, d), lambda i: (i, 0)),
        out_shape=jax.ShapeDtypeStruct((t, d), F32),
        scratch_shapes=[pltpu.VMEM((tm, D_FF), BF16)],
        compiler_params=pltpu.CompilerParams(
            dimension_semantics=("arbitrary",),
            vmem_limit_bytes=VMEM_LIMIT_BYTES),
        name="ffn_ln2",
    )(x1, mod, w_gate, w_up, w_down, ln_g, ln_b)


def _rotary_tables(seq):
    d = RET_HEAD_DIM
    inv_freq = ROPE_BASE ** (-np.arange(0, d, 2, dtype=np.float64) / d)
    ang = np.arange(seq, dtype=np.float64)[:, None] * inv_freq[None, :]
    cos, sin = np.cos(ang), np.sin(ang)
    cosq = np.concatenate([cos, cos], axis=-1)
    sinq = np.concatenate([-sin, sin], axis=-1)
    kscale = d ** -0.5
    return tuple(jnp.asarray(t.astype(np.float32)) for t in (cosq, sinq, cosq * kscale, sinq * kscale))


def _retention_consts():
    h, L, d = RET_HEADS, RET_CHUNK, RET_HEAD_DIM
    log_g = np.log1p(-(2.0 ** (-5.0 - np.arange(h, dtype=np.float64))))
    idx = np.arange(L, dtype=np.float64)
    diff = idx[:, None] - idx[None, :]
    dmask = np.where(diff >= 0, np.exp(log_g[:, None, None] * np.maximum(diff, 0.0)), 0.0)
    zeta = np.exp(log_g[:, None] * (L - 1 - idx)[None, :])
    xi = np.exp(log_g[:, None] * (idx + 1)[None, :])
    cdecay = np.exp(log_g * L)
    bcast = lambda a: np.broadcast_to(a[:, :, None], (h, L, d))
    return tuple(jnp.asarray(np.ascontiguousarray(t).astype(np.float32))
                 for t in (dmask, bcast(xi), bcast(zeta), cdecay))


def kernel(x, c, ada_w, ada_b, w_in, conv_w, w_out, ln1_g, ln1_b, w_gate, w_up, w_down, ln2_g, ln2_b):
    bsz, seq, d = x.shape
    assert ada_w.shape[0] == DEPTH
    assert seq % SEQ_TILE == 0 and seq % FFN_TILE == 0 and D_FF % FF_CHUNK == 0
    assert SEQ_TILE % SUB_ROWS == 0 and FFN_TILE % FFN_SUB_ROWS == 0 and SUB_ROWS % RET_CHUNK == 0
    tables = _rotary_tables(seq)
    consts = _retention_consts()
    for layer in range(DEPTH):
        mod = _adaln(c, ada_w[layer], ada_b[layer])
        x1 = _mixer(x, mod, w_in[layer].astype(BF16), conv_w[layer], w_out[layer].astype(BF16),
                    ln1_g[layer].reshape(1, d), ln1_b[layer].reshape(1, d), tables, consts)
        x2 = _ffn(x1.reshape(bsz * seq, d), mod, w_gate[layer].astype(BF16), w_up[layer].astype(BF16),
                  w_down[layer].astype(BF16), ln2_g[layer].reshape(1, d), ln2_b[layer].reshape(1, d), seq)
        x = x2.reshape(bsz, seq, d)
    return x
```

```python
import functools

import jax
import jax.numpy as jnp
import numpy as np
from jax import lax
from jax.experimental import pallas as pl
from jax.experimental.pallas import tpu as pltpu

D_MODEL = 1024
CONV_WIDTH = 512
CONV_K = 3
RET_WIDTH = 512
RET_HEADS = 4
RET_HEAD_DIM = 128
RET_CHUNK = 128
ROPE_BASE = 10000.0
N_IN_COLS = 3 * CONV_WIDTH + 4 * RET_WIDTH
D_FF = 2816
DEPTH = 1
DN_ALPHA = float((2 * DEPTH) ** 0.25)
LN_EPS = 1e-5
N_MOD = 6

SEQ_TILE = 512
FFN_TILE = 512
SUB_ROWS = 256
FFN_SUB_ROWS = 256
FF_CHUNK = 256
HALO = 8
BF16_SUBLANES = 16
LANES = 128
ADALN_STEPS = 8
VMEM_LIMIT_BYTES = 56 * 1024 * 1024

F32 = jnp.float32
BF16 = jnp.bfloat16


def _layernorm(v, g, b):
    mu = jnp.mean(v, axis=-1, keepdims=True)
    d = v - mu
    var = jnp.mean(d * d, axis=-1, keepdims=True)
    return d * lax.rsqrt(var + LN_EPS) * g + b


def _dot(a, b):
    return jnp.dot(a, b, preferred_element_type=F32)


def _adaln_kernel(c_ref, w_ref, b_ref, win_ref, wout_ref, o_ref, win_bf_ref, wout_bf_ref):
    sc = jax.nn.silu(c_ref[...])
    o_ref[...] = _dot(sc.astype(BF16), w_ref[...].astype(BF16)) + b_ref[...]
    win_bf_ref[...] = win_ref[...].astype(BF16)
    wout_bf_ref[...] = wout_ref[...].astype(BF16)


def _adaln(c, ada_w, ada_b, w_in, w_out):
    bsz, d = c.shape
    n = N_MOD * d
    steps = ADALN_STEPS
    tn = n // steps
    rows = d // steps
    assert tn * steps == n and tn % LANES == 0 and rows * steps == d and rows % BF16_SUBLANES == 0
    return pl.pallas_call(
        _adaln_kernel,
        grid=(steps,),
        in_specs=[
            pl.BlockSpec((bsz, d), lambda j: (0, 0)),
            pl.BlockSpec((d, tn), lambda j: (0, j)),
            pl.BlockSpec((1, tn), lambda j: (0, j)),
            pl.BlockSpec((rows, N_IN_COLS), lambda j: (j, 0)),
            pl.BlockSpec((rows, d), lambda j: (j, 0)),
        ],
        out_specs=[
            pl.BlockSpec((bsz, tn), lambda j: (0, j)),
            pl.BlockSpec((rows, N_IN_COLS), lambda j: (j, 0)),
            pl.BlockSpec((rows, d), lambda j: (j, 0)),
        ],
        out_shape=[
            jax.ShapeDtypeStruct((bsz, n), F32),
            jax.ShapeDtypeStruct((d, N_IN_COLS), BF16),
            jax.ShapeDtypeStruct((d, d), BF16),
        ],
        compiler_params=pltpu.CompilerParams(dimension_semantics=("arbitrary",)),
        name="adaln_mod",
    )(c, ada_w, ada_b.reshape(1, n), w_in, w_out)


def _mixer_kernel(x_ref, mod_ref, win_ref, convw_ref, wout_ref, g_ref, b_ref,
                  cosq_ref, sinq_ref, cosk_ref, sink_ref,
                  dmask_ref, xi_ref, zeta_ref, cdecay_ref,
                  wg_ref, wu_ref, wd_ref,
                  o_ref, wg_bf_ref, wu_bf_ref, wd_bf_ref, u_scr, state_scr):
    wg_bf_ref[...] = wg_ref[...].astype(BF16)
    wu_bf_ref[...] = wu_ref[...].astype(BF16)
    wd_bf_ref[...] = wd_ref[...].astype(BF16)

    ts = x_ref.shape[1]
    wc, wr, hd, L = CONV_WIDTH, RET_WIDTH, RET_HEAD_DIM, RET_CHUNK
    b = pl.program_id(0)
    j = pl.program_id(1)

    @pl.when(j == 0)
    def _():
        state_scr[...] = jnp.zeros_like(state_scr)
        u_scr[0:HALO, :] = jnp.zeros((HALO, wc), F32)

    dm = D_MODEL
    shift = mod_ref[pl.ds(b, 1), 0:dm]
    scale = mod_ref[pl.ds(b, 1), dm:2 * dm]
    gate = mod_ref[pl.ds(b, 1), 2 * dm:3 * dm]
    states = [state_scr[hh] for hh in range(RET_HEADS)]
    o = 3 * wc

    for r0 in range(0, ts, SUB_ROWS):
        x = x_ref[0, r0:r0 + SUB_ROWS, :]
        h = (x * (1.0 + scale) + shift).astype(BF16)

        cg = _dot(h, win_ref[:, 0:wc])
        xin = _dot(h, win_ref[:, wc:2 * wc])
        u0 = HALO + r0
        u_scr[u0:u0 + SUB_ROWS, :] = cg * xin
        bg = _dot(h, win_ref[:, 2 * wc:3 * wc])
        conv = (convw_ref[0:1, :] * u_scr[u0 - 2:u0 - 2 + SUB_ROWS, :]
                + convw_ref[1:2, :] * u_scr[u0 - 1:u0 - 1 + SUB_ROWS, :]
                + convw_ref[2:3, :] * u_scr[u0:u0 + SUB_ROWS, :])
        y_conv = (bg * conv).astype(BF16)

        q = _dot(h, win_ref[:, o:o + wr])
        k = _dot(h, win_ref[:, o + wr:o + 2 * wr])
        v = _dot(h, win_ref[:, o + 2 * wr:o + 3 * wr])
        unit = {}
        for c in range(SUB_ROWS // L):
            rows = slice(c * L, (c + 1) * L)
            trow = slice(r0 + c * L, r0 + (c + 1) * L)
            cq, sq = cosq_ref[trow, :], sinq_ref[trow, :]
            ck, sk = cosk_ref[trow, :], sink_ref[trow, :]
            for hh in range(RET_HEADS):
                cols = slice(hh * hd, (hh + 1) * hd)
                qf, kf = q[rows, cols], k[rows, cols]
                vb = v[rows, cols].astype(BF16)
                qf = qf * cq + pltpu.roll(qf, hd // 2, 1) * sq
                kf = kf * ck + pltpu.roll(kf, hd // 2, 1) * sk
                s = lax.dot_general(qf.astype(BF16), kf.astype(BF16),
                                    (((1,), (1,)), ((), ())), preferred_element_type=F32)
                kv = lax.dot_general((kf * zeta_ref[hh]).astype(BF16), vb,
                                     (((0,), (0,)), ((), ())), preferred_element_type=F32)
                p = (s * dmask_ref[hh]).astype(BF16)
                qx = (qf * xi_ref[hh]).astype(BF16)
                unit[c, hh] = (jnp.concatenate([p, qx], axis=1), vb, kv)

        r_rows = []
        for c in range(SUB_ROWS // L):
            r_heads = []
            for hh in range(RET_HEADS):
                pq, vb, kv = unit[c, hh]
                out = _dot(pq, jnp.concatenate([vb, states[hh].astype(BF16)], axis=0))
                states[hh] = cdecay_ref[hh] * states[hh] + kv
                mu = jnp.mean(out, axis=-1, keepdims=True)
                dlt = out - mu
                var = jnp.mean(dlt * dlt, axis=-1, keepdims=True)
                r_heads.append(dlt * lax.rsqrt(var + LN_EPS))
            r_rows.append(jnp.concatenate(r_heads, axis=1))
        r = jnp.concatenate(r_rows, axis=0)

        gz = _dot(h, win_ref[:, o + 3 * wr:o + 4 * wr])
        y_ret = (jax.nn.silu(gz) * r).astype(BF16)

        mix = _dot(jnp.concatenate([y_conv, y_ret], axis=1), wout_ref[...])
        o_ref[0, r0:r0 + SUB_ROWS, :] = _layernorm(DN_ALPHA * x + gate * mix, g_ref[...], b_ref[...])

    for hh in range(RET_HEADS):
        state_scr[hh] = states[hh]
    u_scr[0:HALO, :] = u_scr[ts:ts + HALO, :]


def _const_spec(shape, ngrid):
    zeros = (0,) * len(shape)
    if ngrid == 1:
        return pl.BlockSpec(shape, lambda i: zeros, pipeline_mode=pl.Buffered(1))
    return pl.BlockSpec(shape, lambda b, j: zeros, pipeline_mode=pl.Buffered(1))


def _mixer(x, mod, w_in, conv_w, w_out, ln_g, ln_b, tables, consts, w_gate, w_up, w_down):
    bsz, seq, d = x.shape
    ts = SEQ_TILE
    tps = seq // ts
    nsteps = bsz * tps
    cosq, sinq, cosk, sink = tables
    dmask, xi_b, zeta_b, cdecay = consts
    hd = RET_HEAD_DIM
    tab_spec = pl.BlockSpec((ts, hd), lambda b, j: (j, 0))
    gu_rows = d // nsteps
    dn_rows = D_FF // (nsteps // 2)
    assert gu_rows * nsteps == d and dn_rows * (nsteps // 2) == D_FF
    assert gu_rows % BF16_SUBLANES == 0 and dn_rows % BF16_SUBLANES == 0
    gu_spec = pl.BlockSpec((gu_rows, D_FF), lambda b, j: (b * tps + j, 0))
    dn_spec = pl.BlockSpec((dn_rows, d), lambda b, j: ((b * tps + j) // 2, 0))
    return pl.pallas_call(
        _mixer_kernel,
        grid=(bsz, tps),
        in_specs=[
            pl.BlockSpec((1, ts, d), lambda b, j: (b, j, 0)),
            _const_spec((bsz, N_MOD * d), 2),
            _const_spec((d, N_IN_COLS), 2),
            _const_spec((CONV_K, CONV_WIDTH), 2),
            _const_spec((d, d), 2),
            _const_spec((1, d), 2),
            _const_spec((1, d), 2),
            tab_spec, tab_spec, tab_spec, tab_spec,
            _const_spec((RET_HEADS, RET_CHUNK, RET_CHUNK), 2),
            _const_spec((RET_HEADS, RET_CHUNK, hd), 2),
            _const_spec((RET_HEADS, RET_CHUNK, hd), 2),
            pl.BlockSpec(memory_space=pltpu.SMEM),
            gu_spec, gu_spec, dn_spec,
        ],
        out_specs=[
            pl.BlockSpec((1, ts, d), lambda b, j: (b, j, 0)),
            gu_spec, gu_spec, dn_spec,
        ],
        out_shape=[
            jax.ShapeDtypeStruct((bsz, seq, d), F32),
            jax.ShapeDtypeStruct((d, D_FF), BF16),
            jax.ShapeDtypeStruct((d, D_FF), BF16),
            jax.ShapeDtypeStruct((D_FF, d), BF16),
        ],
        scratch_shapes=[
            pltpu.VMEM((ts + HALO, CONV_WIDTH), F32),
            pltpu.VMEM((RET_HEADS, hd, hd), F32),
        ],
        compiler_params=pltpu.CompilerParams(
            dimension_semantics=("arbitrary", "arbitrary"),
            vmem_limit_bytes=VMEM_LIMIT_BYTES),
        name="mixer_ln1",
    )(x, mod, w_in, conv_w, w_out, ln_g, ln_b, cosq, sinq, cosk, sink,
      dmask, xi_b, zeta_b, cdecay, w_gate, w_up, w_down)


def _ffn_kernel(tiles_per_seq, x_ref, mod_ref, wg_ref, wu_ref, wd_ref, g_ref, b_ref, o_ref, act_scr):
    tm = x_ref.shape[0]
    b = pl.program_id(0) // tiles_per_seq
    dm = D_MODEL
    shift = mod_ref[pl.ds(b, 1), 3 * dm:4 * dm]
    scale = mod_ref[pl.ds(b, 1), 4 * dm:5 * dm]
    gate = mod_ref[pl.ds(b, 1), 5 * dm:6 * dm]
    fc = FF_CHUNK
    for r0 in range(0, tm, FFN_SUB_ROWS):
        rows = slice(r0, r0 + FFN_SUB_ROWS)
        x = x_ref[rows, :]
        h = (x * (1.0 + scale) + shift).astype(BF16)
        for c in range(D_FF // fc):
            cols = slice(c * fc, (c + 1) * fc)
            act_scr[rows, cols] = (jax.nn.silu(_dot(h, wg_ref[:, cols])) * _dot(h, wu_ref[:, cols])).astype(BF16)
        ff = _dot(act_scr[rows, :], wd_ref[...])
        o_ref[rows, :] = _layernorm(DN_ALPHA * x + gate * ff, g_ref[...], b_ref[...])


def _ffn(x1, mod, w_gate, w_up, w_down, ln_g, ln_b, seq):
    t, d = x1.shape
    bsz = t // seq
    tm = FFN_TILE
    return pl.pallas_call(
        functools.partial(_ffn_kernel, seq // tm),
        grid=(t // tm,),
        in_specs=[
            pl.BlockSpec((tm, d), lambda i: (i, 0)),
            _const_spec((bsz, N_MOD * d), 1),
            _const_spec((d, D_FF), 1),
            _const_spec((d, D_FF), 1),
            _const_spec((D_FF, d), 1),
            _const_spec((1, d), 1),
            _const_spec((1, d), 1),
        ],
        out_specs=pl.BlockSpec((tm, d), lambda i: (i, 0)),
        out_shape=jax.ShapeDtypeStruct((t, d), F32),
        scratch_shapes=[pltpu.VMEM((tm, D_FF), BF16)],
        compiler_params=pltpu.CompilerParams(
            dimension_semantics=("arbitrary",),
            vmem_limit_bytes=VMEM_LIMIT_BYTES),
        name="ffn_ln2",
    )(x1, mod, w_gate, w_up, w_down, ln_g, ln_b)


def _rotary_tables(seq):
    d = RET_HEAD_DIM
    inv_freq = ROPE_BASE ** (-np.arange(0, d, 2, dtype=np.float64) / d)
    ang = np.arange(seq, dtype=np.float64)[:, None] * inv_freq[None, :]
    cos, sin = np.cos(ang), np.sin(ang)
    cosq = np.concatenate([cos, cos], axis=-1)
    sinq = np.concatenate([-sin, sin], axis=-1)
    kscale = d ** -0.5
    return tuple(jnp.asarray(t.astype(np.float32)) for t in (cosq, sinq, cosq * kscale, sinq * kscale))


def _retention_consts():
    h, L, d = RET_HEADS, RET_CHUNK, RET_HEAD_DIM
    log_g = np.log1p(-(2.0 ** (-5.0 - np.arange(h, dtype=np.float64))))
    idx = np.arange(L, dtype=np.float64)
    diff = idx[:, None] - idx[None, :]
    dmask = np.where(diff >= 0, np.exp(log_g[:, None, None] * np.maximum(diff, 0.0)), 0.0)
    zeta = np.exp(log_g[:, None] * (L - 1 - idx)[None, :])
    xi = np.exp(log_g[:, None] * (idx + 1)[None, :])
    cdecay = np.exp(log_g * L)
    bcast = lambda a: np.broadcast_to(a[:, :, None], (h, L, d))
    return tuple(jnp.asarray(np.ascontiguousarray(t).astype(np.float32))
                 for t in (dmask, bcast(xi), bcast(zeta), cdecay))


def kernel(x, c, ada_w, ada_b, w_in, conv_w, w_out, ln1_g, ln1_b, w_gate, w_up, w_down, ln2_g, ln2_b):
    bsz, seq, d = x.shape
    assert ada_w.shape[0] == DEPTH
    assert seq % SEQ_TILE == 0 and seq % FFN_TILE == 0 and D_FF % FF_CHUNK == 0
    assert SEQ_TILE % SUB_ROWS == 0 and FFN_TILE % FFN_SUB_ROWS == 0 and SUB_ROWS % RET_CHUNK == 0
    tables = _rotary_tables(seq)
    consts = _retention_consts()
    for layer in range(DEPTH):
        mod, win_bf, wout_bf = _adaln(c, ada_w[layer], ada_b[layer], w_in[layer], w_out[layer])
        x1, wg_bf, wu_bf, wd_bf = _mixer(
            x, mod, win_bf, conv_w[layer], wout_bf,
            ln1_g[layer].reshape(1, d), ln1_b[layer].reshape(1, d), tables, consts,
            w_gate[layer], w_up[layer], w_down[layer])
        x2 = _ffn(x1.reshape(bsz * seq, d), mod, wg_bf, wu_bf, wd_bf,
                  ln2_g[layer].reshape(1, d), ln2_b[layer].reshape(1, d), seq)
        x = x2.reshape(bsz, seq, d)
    return x
```

```python
import functools

import jax
import jax.numpy as jnp
import numpy as np
from jax import lax
from jax.experimental import pallas as pl
from jax.experimental.pallas import tpu as pltpu

D_MODEL = 1024
CONV_WIDTH = 512
CONV_K = 3
RET_WIDTH = 512
RET_HEADS = 4
RET_HEAD_DIM = 128
RET_CHUNK = 128
ROPE_BASE = 10000.0
N_IN_COLS = 3 * CONV_WIDTH + 4 * RET_WIDTH
D_FF = 2816
DEPTH = 1
DN_ALPHA = float((2 * DEPTH) ** 0.25)
LN_EPS = 1e-5
N_MOD = 6

SEQ_TILE = 512
FFN_TILE = 512
SUB_ROWS = 256
FFN_SUB_ROWS = 256
FF_CHUNK = 256
HALO = 8
BF16_SUBLANES = 16
LANES = 128
ADALN_STEPS = 8
VMEM_LIMIT_BYTES = 56 * 1024 * 1024

F32 = jnp.float32
BF16 = jnp.bfloat16


def _layernorm(v, g, b):
    mu = jnp.mean(v, axis=-1, keepdims=True)
    d = v - mu
    var = jnp.mean(d * d, axis=-1, keepdims=True)
    return d * lax.rsqrt(var + LN_EPS) * g + b


def _dot(a, b):
    return jnp.dot(a, b, preferred_element_type=F32)


def _const_spec(shape):
    zeros = (0,) * len(shape)
    return pl.BlockSpec(shape, lambda i: zeros, pipeline_mode=pl.Buffered(1))


def _adaln_kernel(c_ref, w_ref, b_ref, win_ref, wout_ref, o_ref, win_bf_ref, wout_bf_ref):
    sc = jax.nn.silu(c_ref[...])
    o_ref[...] = _dot(sc.astype(BF16), w_ref[...].astype(BF16)) + b_ref[...]
    win_bf_ref[...] = win_ref[...].astype(BF16)
    wout_bf_ref[...] = wout_ref[...].astype(BF16)


def _adaln(c, ada_w, ada_b, w_in, w_out):
    bsz, d = c.shape
    n = N_MOD * d
    steps = ADALN_STEPS
    tn = n // steps
    rows = d // steps
    assert tn * steps == n and tn % LANES == 0 and rows * steps == d and rows % BF16_SUBLANES == 0
    return pl.pallas_call(
        _adaln_kernel,
        grid=(steps,),
        in_specs=[
            pl.BlockSpec((bsz, d), lambda j: (0, 0)),
            pl.BlockSpec((d, tn), lambda j: (0, j)),
            pl.BlockSpec((1, tn), lambda j: (0, j)),
            pl.BlockSpec((rows, N_IN_COLS), lambda j: (j, 0)),
            pl.BlockSpec((rows, d), lambda j: (j, 0)),
        ],
        out_specs=[
            pl.BlockSpec((bsz, tn), lambda j: (0, j)),
            pl.BlockSpec((rows, N_IN_COLS), lambda j: (j, 0)),
            pl.BlockSpec((rows, d), lambda j: (j, 0)),
        ],
        out_shape=[
            jax.ShapeDtypeStruct((bsz, n), F32),
            jax.ShapeDtypeStruct((d, N_IN_COLS), BF16),
            jax.ShapeDtypeStruct((d, d), BF16),
        ],
        compiler_params=pltpu.CompilerParams(dimension_semantics=("arbitrary",)),
        name="adaln_mod",
    )(c, ada_w, ada_b.reshape(1, n), w_in, w_out)


def _mixer_kernel(n_tiles, tps, x_ref, mod_ref, win_ref, convw_ref, wout_ref, g_ref, b_ref,
                  cosq_ref, sinq_ref, cosk_ref, sink_ref,
                  dmask_ref, xi_ref, zeta_ref, cdecay_ref,
                  wg_ref, wu_ref, wd_ref,
                  o_ref, wg_bf_ref, wu_bf_ref, wd_bf_ref,
                  z_scr, xres_scr, u_scr, state_scr):
    wg_bf_ref[...] = wg_ref[...].astype(BF16)
    wu_bf_ref[...] = wu_ref[...].astype(BF16)
    wd_bf_ref[...] = wd_ref[...].astype(BF16)

    ts = x_ref.shape[1]
    wc, wr, hd, L = CONV_WIDTH, RET_WIDTH, RET_HEAD_DIM, RET_CHUNK
    dm = D_MODEL
    gw = wc
    n_groups = N_IN_COLS // gw
    i = pl.program_id(0)

    def in_proj_stage(r0):
        b = jnp.minimum(i, n_tiles - 1) // tps
        rows = slice(r0, r0 + SUB_ROWS)
        x = x_ref[0, rows, :]
        h = (x * (1.0 + mod_ref[pl.ds(b, 1), dm:2 * dm]) + mod_ref[pl.ds(b, 1), 0:dm]).astype(BF16)

        def group(kk):
            cols = slice(kk * gw, (kk + 1) * gw)
            z_scr[rows, cols] = _dot(h, win_ref[:, cols])

        def save_residual():
            xres_scr[rows, :] = DN_ALPHA * x

        return [functools.partial(group, kk) for kk in range(n_groups)], save_residual

    def mix_stage(r0, states):
        bp = jnp.maximum(i - 1, 0) // tps
        rows = slice(r0, r0 + SUB_ROWS)
        st = {}

        def load_and_rotate():
            u0 = HALO + r0
            u_scr[u0:u0 + SUB_ROWS, :] = z_scr[rows, 0:wc] * z_scr[rows, wc:2 * wc]
            conv = (convw_ref[0, 0:1, :] * u_scr[u0 - 2:u0 - 2 + SUB_ROWS, :]
                    + convw_ref[0, 1:2, :] * u_scr[u0 - 1:u0 - 1 + SUB_ROWS, :]
                    + convw_ref[0, 2:3, :] * u_scr[u0:u0 + SUB_ROWS, :])
            st["y_conv"] = (z_scr[rows, 2 * wc:3 * wc] * conv).astype(BF16)
            o = 3 * wc
            for c in range(SUB_ROWS // L):
                crow = slice(r0 + c * L, r0 + (c + 1) * L)
                cq, sq = cosq_ref[crow, :], sinq_ref[crow, :]
                ck, sk = cosk_ref[crow, :], sink_ref[crow, :]
                for hh in range(RET_HEADS):
                    qf = z_scr[crow, o + hh * hd:o + (hh + 1) * hd]
                    kf = z_scr[crow, o + wr + hh * hd:o + wr + (hh + 1) * hd]
                    vb = z_scr[crow, o + 2 * wr + hh * hd:o + 2 * wr + (hh + 1) * hd].astype(BF16)
                    qf = qf * cq + pltpu.roll(qf, hd // 2, 1) * sq
                    kf = kf * ck + pltpu.roll(kf, hd // 2, 1) * sk
                    st[c, hh] = dict(qb=qf.astype(BF16), kb=kf.astype(BF16),
                                     qx=(qf * xi_ref[hh]).astype(BF16),
                                     kz=(kf * zeta_ref[hh]).astype(BF16), vb=vb)

        def scores_and_kv():
            for c in range(SUB_ROWS // L):
                for hh in range(RET_HEADS):
                    e = st[c, hh]
                    e["s"] = lax.dot_general(e["qb"], e["kb"], (((1,), (1,)), ((), ())),
                                             preferred_element_type=F32)
                    e["kv"] = lax.dot_general(e["kz"], e["vb"], (((0,), (0,)), ((), ())),
                                              preferred_element_type=F32)

        def retention_out():
            r_rows = []
            for c in range(SUB_ROWS // L):
                r_heads = []
                for hh in range(RET_HEADS):
                    e = st[c, hh]
                    p = (e["s"] * dmask_ref[hh]).astype(BF16)
                    out = _dot(jnp.concatenate([p, e["qx"]], axis=1),
                               jnp.concatenate([e["vb"], states[hh].astype(BF16)], axis=0))
                    states[hh] = cdecay_ref[hh] * states[hh] + e["kv"]
                    mu = jnp.mean(out, axis=-1, keepdims=True)
                    dlt = out - mu
                    var = jnp.mean(dlt * dlt, axis=-1, keepdims=True)
                    r_heads.append(dlt * lax.rsqrt(var + LN_EPS))
                r_rows.append(jnp.concatenate(r_heads, axis=1))
            st["r"] = jnp.concatenate(r_rows, axis=0)

        def out_proj():
            gz = z_scr[rows, 3 * wc + 3 * wr:3 * wc + 4 * wr]
            y_ret = (jax.nn.silu(gz) * st["r"]).astype(BF16)
            st["mix"] = _dot(jnp.concatenate([st["y_conv"], y_ret], axis=1), wout_ref[...])

        def norm_store():
            gate = mod_ref[pl.ds(bp, 1), 2 * dm:3 * dm]
            o_ref[0, rows, :] = _layernorm(xres_scr[rows, :] + gate * st["mix"], g_ref[...], b_ref[...])

        return load_and_rotate, scores_and_kv, retention_out, out_proj, norm_store

    def mix_prologue():
        jp = jnp.maximum(i - 1, 0) % tps

        @pl.when(jp == 0)
        def _():
            state_scr[...] = jnp.zeros_like(state_scr)
            u_scr[0:HALO, :] = jnp.zeros((HALO, wc), F32)

        return [state_scr[hh] for hh in range(RET_HEADS)]

    def mix_epilogue(states):
        for hh in range(RET_HEADS):
            state_scr[hh] = states[hh]
        u_scr[0:HALO, :] = u_scr[ts:ts + HALO, :]

    @pl.when(i == 0)
    def _():
        for r0 in range(0, ts, SUB_ROWS):
            groups, save_residual = in_proj_stage(r0)
            for g in groups:
                g()
            save_residual()

    @pl.when(i == n_tiles)
    def _():
        states = mix_prologue()
        for r0 in range(0, ts, SUB_ROWS):
            for stage in mix_stage(r0, states):
                stage()
        mix_epilogue(states)

    @pl.when((i > 0) & (i < n_tiles))
    def _():
        states = mix_prologue()
        for r0 in range(0, ts, SUB_ROWS):
            groups, save_residual = in_proj_stage(r0)
            load_and_rotate, scores_and_kv, retention_out, out_proj, norm_store = mix_stage(r0, states)
            load_and_rotate()
            groups[0]()
            groups[1]()
            scores_and_kv()
            groups[2]()
            groups[3]()
            retention_out()
            groups[4]()
            groups[5]()
            out_proj()
            groups[6]()
            norm_store()
            save_residual()
        mix_epilogue(states)


def _mixer(x, mod, w_in, conv_w, w_out, ln_g, ln_b, tables, consts, w_gate, w_up, w_down):
    bsz, seq, d = x.shape
    ts = SEQ_TILE
    tps = seq // ts
    n_tiles = bsz * tps
    cosq, sinq, cosk, sink = tables
    dmask, xi_b, zeta_b, cdecay = consts
    hd = RET_HEAD_DIM
    cur = lambda i: jnp.minimum(i, n_tiles - 1)
    prev = lambda i: jnp.maximum(i - 1, 0)
    tab_spec = pl.BlockSpec((ts, hd), lambda i: (prev(i) % tps, 0))
    gu_rows = d // n_tiles
    dn_rows = D_FF // (n_tiles // 2)
    assert gu_rows * n_tiles == d and dn_rows * (n_tiles // 2) == D_FF
    assert gu_rows % BF16_SUBLANES == 0 and dn_rows % BF16_SUBLANES == 0
    gu_spec = pl.BlockSpec((gu_rows, D_FF), lambda i: (cur(i), 0))
    dn_spec = pl.BlockSpec((dn_rows, d), lambda i: (cur(i) // 2, 0))
    return pl.pallas_call(
        functools.partial(_mixer_kernel, n_tiles, tps),
        grid=(n_tiles + 1,),
        in_specs=[
            pl.BlockSpec((1, ts, d), lambda i: (cur(i) // tps, cur(i) % tps, 0)),
            _const_spec((bsz, N_MOD * d)),
            _const_spec((d, N_IN_COLS)),
            _const_spec((1, CONV_K, CONV_WIDTH)),
            _const_spec((d, d)),
            _const_spec((1, d)),
            _const_spec((1, d)),
            tab_spec, tab_spec, tab_spec, tab_spec,
            _const_spec((RET_HEADS, RET_CHUNK, RET_CHUNK)),
            _const_spec((RET_HEADS, RET_CHUNK, hd)),
            _const_spec((RET_HEADS, RET_CHUNK, hd)),
            pl.BlockSpec(memory_space=pltpu.SMEM),
            gu_spec, gu_spec, dn_spec,
        ],
        out_specs=[
            pl.BlockSpec((1, ts, d), lambda i: (prev(i) // tps, prev(i) % tps, 0)),
            gu_spec, gu_spec, dn_spec,
        ],
        out_shape=[
            jax.ShapeDtypeStruct((bsz, seq, d), F32),
            jax.ShapeDtypeStruct((d, D_FF), BF16),
            jax.ShapeDtypeStruct((d, D_FF), BF16),
            jax.ShapeDtypeStruct((D_FF, d), BF16),
        ],
        scratch_shapes=[
            pltpu.VMEM((ts, N_IN_COLS), F32),
            pltpu.VMEM((ts, d), F32),
            pltpu.VMEM((ts + HALO, CONV_WIDTH), F32),
            pltpu.VMEM((RET_HEADS, hd, hd), F32),
        ],
        compiler_params=pltpu.CompilerParams(
            dimension_semantics=("arbitrary",),
            vmem_limit_bytes=VMEM_LIMIT_BYTES),
        name="mixer_ln1",
    )(x, mod, w_in, conv_w, w_out, ln_g, ln_b, cosq, sinq, cosk, sink,
      dmask, xi_b, zeta_b, cdecay, w_gate, w_up, w_down)


def _ffn_kernel(n_tiles, tiles_per_seq, x_ref, mod_ref, wg_ref, wu_ref, wd_ref, g_ref, b_ref,
                o_ref, act_scr, xres_scr):
    tm = x_ref.shape[0]
    i = pl.program_id(0)
    dm = D_MODEL
    fc = FF_CHUNK

    def down_and_norm():
        bp = jnp.maximum(i - 1, 0) // tiles_per_seq
        gate = mod_ref[pl.ds(bp, 1), 5 * dm:6 * dm]
        for r0 in range(0, tm, FFN_SUB_ROWS):
            rows = slice(r0, r0 + FFN_SUB_ROWS)
            ff = _dot(act_scr[rows, :], wd_ref[...])
            o_ref[rows, :] = _layernorm(xres_scr[rows, :] + gate * ff, g_ref[...], b_ref[...])

    def gate_up():
        b = jnp.minimum(i, n_tiles - 1) // tiles_per_seq
        shift = mod_ref[pl.ds(b, 1), 3 * dm:4 * dm]
        scale = mod_ref[pl.ds(b, 1), 4 * dm:5 * dm]
        for r0 in range(0, tm, FFN_SUB_ROWS):
            rows = slice(r0, r0 + FFN_SUB_ROWS)
            x = x_ref[rows, :]
            h = (x * (1.0 + scale) + shift).astype(BF16)
            for c in range(D_FF // fc):
                cols = slice(c * fc, (c + 1) * fc)
                act_scr[rows, cols] = (jax.nn.silu(_dot(h, wg_ref[:, cols]))
                                       * _dot(h, wu_ref[:, cols])).astype(BF16)
            xres_scr[rows, :] = DN_ALPHA * x

    @pl.when(i == 0)
    def _():
        gate_up()

    @pl.when(i == n_tiles)
    def _():
        down_and_norm()

    @pl.when((i > 0) & (i < n_tiles))
    def _():
        down_and_norm()
        gate_up()


def _ffn(x1, mod, w_gate, w_up, w_down, ln_g, ln_b, seq):
    t, d = x1.shape
    bsz = t // seq
    tm = FFN_TILE
    n_tiles = t // tm
    return pl.pallas_call(
        functools.partial(_ffn_kernel, n_tiles, seq // tm),
        grid=(n_tiles + 1,),
        in_specs=[
            pl.BlockSpec((tm, d), lambda i: (jnp.minimum(i, n_tiles - 1), 0)),
            _const_spec((bsz, N_MOD * d)),
            _const_spec((d, D_FF)),
            _const_spec((d, D_FF)),
            _const_spec((D_FF, d)),
            _const_spec((1, d)),
            _const_spec((1, d)),
        ],
        out_specs=pl.BlockSpec((tm, d), lambda i: (jnp.maximum(i - 1, 0), 0)),
        out_shape=jax.ShapeDtypeStruct((t, d), F32),
        scratch_shapes=[pltpu.VMEM((tm, D_FF), BF16), pltpu.VMEM((tm, d), F32)],
        compiler_params=pltpu.CompilerParams(
            dimension_semantics=("arbitrary",),
            vmem_limit_bytes=VMEM_LIMIT_BYTES),
        name="ffn_ln2",
    )(x1, mod, w_gate, w_up, w_down, ln_g, ln_b)


def _rotary_tables(seq):
    d = RET_HEAD_DIM
    inv_freq = ROPE_BASE ** (-np.arange(0, d, 2, dtype=np.float64) / d)
    ang = np.arange(seq, dtype=np.float64)[:, None] * inv_freq[None, :]
    cos, sin = np.cos(ang), np.sin(ang)
    cosq = np.concatenate([cos, cos], axis=-1)
    sinq = np.concatenate([-sin, sin], axis=-1)
    kscale = d ** -0.5
    return tuple(jnp.asarray(t.astype(np.float32)) for t in (cosq, sinq, cosq * kscale, sinq * kscale))


def _retention_consts():
    h, L, d = RET_HEADS, RET_CHUNK, RET_HEAD_DIM
    log_g = np.log1p(-(2.0 ** (-5.0 - np.arange(h, dtype=np.float64))))
    idx = np.arange(L, dtype=np.float64)
    diff = idx[:, None] - idx[None, :]
    dmask = np.where(diff >= 0, np.exp(log_g[:, None, None] * np.maximum(diff, 0.0)), 0.0)
    zeta = np.exp(log_g[:, None] * (L - 1 - idx)[None, :])
    xi = np.exp(log_g[:, None] * (idx + 1)[None, :])
    cdecay = np.exp(log_g * L)
    bcast = lambda a: np.broadcast_to(a[:, :, None], (h, L, d))
    return tuple(jnp.asarray(np.ascontiguousarray(t).astype(np.float32))
                 for t in (dmask, bcast(xi), bcast(zeta), cdecay))


def kernel(x, c, ada_w, ada_b, w_in, conv_w, w_out, ln1_g, ln1_b, w_gate, w_up, w_down, ln2_g, ln2_b):
    bsz, seq, d = x.shape
    assert ada_w.shape[0] == DEPTH
    assert seq % SEQ_TILE == 0 and seq % FFN_TILE == 0 and D_FF % FF_CHUNK == 0
    assert SEQ_TILE % SUB_ROWS == 0 and FFN_TILE % FFN_SUB_ROWS == 0 and SUB_ROWS % RET_CHUNK == 0
    tables = _rotary_tables(seq)
    consts = _retention_consts()
    for layer in range(DEPTH):
        mod, win_bf, wout_bf = _adaln(c, ada_w[layer], ada_b[layer], w_in[layer], w_out[layer])
        x1, wg_bf, wu_bf, wd_bf = _mixer(
            x, mod, win_bf, conv_w, wout_bf,
            ln1_g[layer].reshape(1, d), ln1_b[layer].reshape(1, d), tables, consts,
            w_gate[layer], w_up[layer], w_down[layer])
        x2 = _ffn(x1.reshape(bsz * seq, d), mod, wg_bf, wu_bf, wd_bf,
                  ln2_g[layer].reshape(1, d), ln2_b[layer].reshape(1, d), seq)
        x = x2.reshape(bsz, seq, d)
    return x
```

```python
import functools

import jax
import jax.numpy as jnp
import numpy as np
from jax import lax
from jax.experimental import pallas as pl
from jax.experimental.pallas import tpu as pltpu

D_MODEL = 1024
CONV_WIDTH = 512
CONV_K = 3
RET_WIDTH = 512
RET_HEADS = 4
RET_HEAD_DIM = 128
RET_CHUNK = 128
ROPE_BASE = 10000.0
N_IN_COLS = 3 * CONV_WIDTH + 4 * RET_WIDTH
D_FF = 2816
DEPTH = 1
DN_ALPHA = float((2 * DEPTH) ** 0.25)
LN_EPS = 1e-5
N_MOD = 6

SEQ_TILE = 512
FFN_TILE = 512
SUB_ROWS = 256
FFN_SUB_ROWS = 256
FF_CHUNK = 256
IN_GROUP = 512
WAIT_STARTS = (0, 1, 3, 6)
HALO = 8
BF16_SUBLANES = 16
LANES = 128
ADALN_STEPS = 8
VMEM_LIMIT_BYTES = 56 * 1024 * 1024

F32 = jnp.float32
BF16 = jnp.bfloat16


def _layernorm(v, g, b):
    mu = jnp.mean(v, axis=-1, keepdims=True)
    d = v - mu
    var = jnp.mean(d * d, axis=-1, keepdims=True)
    return d * lax.rsqrt(var + LN_EPS) * g + b


def _dot(a, b):
    return jnp.dot(a, b, preferred_element_type=F32)


def _wait_batches(n):
    starts = [k for k in WAIT_STARTS if k < n]
    return {k: e for k, e in zip(starts, starts[1:] + [n])}


def _const_spec(shape):
    zeros = (0,) * len(shape)
    return pl.BlockSpec(shape, lambda i: zeros, pipeline_mode=pl.Buffered(1))


def _adaln_kernel(c_ref, w_ref, b_ref, win_ref, wout_ref, o_ref, win_bf_ref, wout_bf_ref):
    sc = jax.nn.silu(c_ref[...])
    o_ref[...] = _dot(sc.astype(BF16), w_ref[...].astype(BF16)) + b_ref[...]
    for kk in range(N_IN_COLS // IN_GROUP):
        win_bf_ref[kk] = win_ref[:, kk * IN_GROUP:(kk + 1) * IN_GROUP].astype(BF16)
    wout_bf_ref[...] = wout_ref[...].astype(BF16)


def _adaln(c, ada_w, ada_b, w_in, w_out):
    bsz, d = c.shape
    n = N_MOD * d
    steps = ADALN_STEPS
    tn = n // steps
    rows = d // steps
    assert tn * steps == n and tn % LANES == 0 and rows * steps == d and rows % BF16_SUBLANES == 0
    return pl.pallas_call(
        _adaln_kernel,
        grid=(steps,),
        in_specs=[
            pl.BlockSpec((bsz, d), lambda j: (0, 0)),
            pl.BlockSpec((d, tn), lambda j: (0, j)),
            pl.BlockSpec((1, tn), lambda j: (0, j)),
            pl.BlockSpec((rows, N_IN_COLS), lambda j: (j, 0)),
            pl.BlockSpec((rows, d), lambda j: (j, 0)),
        ],
        out_specs=[
            pl.BlockSpec((bsz, tn), lambda j: (0, j)),
            pl.BlockSpec((N_IN_COLS // IN_GROUP, rows, IN_GROUP), lambda j: (0, j, 0)),
            pl.BlockSpec((rows, d), lambda j: (j, 0)),
        ],
        out_shape=[
            jax.ShapeDtypeStruct((bsz, n), F32),
            jax.ShapeDtypeStruct((N_IN_COLS // IN_GROUP, d, IN_GROUP), BF16),
            jax.ShapeDtypeStruct((d, d), BF16),
        ],
        compiler_params=pltpu.CompilerParams(dimension_semantics=("arbitrary",)),
        name="adaln_mod",
    )(c, ada_w, ada_b.reshape(1, n), w_in, w_out)


def _mixer_kernel(n_tiles, tps, x_ref, mod_ref, win_hbm, convw_ref, wout_hbm, g_ref, b_ref,
                  cosq_ref, sinq_ref, cosk_ref, sink_ref,
                  dmask_ref, xi_ref, zeta_ref, cdecay_ref,
                  wg_ref, wu_ref, wd_ref,
                  o_ref, wg_bf_ref, wu_bf_ref, wd_bf_ref,
                  z_scr, xres_scr, u_scr, state_scr, win_ref, wout_ref, wsem):
    for c in range(D_FF // FF_CHUNK):
        wg_bf_ref[c] = wg_ref[:, c * FF_CHUNK:(c + 1) * FF_CHUNK].astype(BF16)
        wu_bf_ref[c] = wu_ref[:, c * FF_CHUNK:(c + 1) * FF_CHUNK].astype(BF16)
    wd_bf_ref[...] = wd_ref[...].astype(BF16)

    ts = x_ref.shape[1]
    wc, wr, hd, L = CONV_WIDTH, RET_WIDTH, RET_HEAD_DIM, RET_CHUNK
    dm = D_MODEL
    gw = IN_GROUP
    n_groups = N_IN_COLS // gw
    batches = _wait_batches(n_groups)
    i = pl.program_id(0)

    def win_copy(kk):
        return pltpu.make_async_copy(win_hbm.at[kk], win_ref.at[kk], wsem.at[kk])

    def wout_copy():
        return pltpu.make_async_copy(wout_hbm, wout_ref, wsem.at[n_groups])

    def in_proj_stage(r0):
        b = jnp.minimum(i, n_tiles - 1) // tps
        rows = slice(r0, r0 + SUB_ROWS)
        x = x_ref[0, rows, :]
        h = (x * (1.0 + mod_ref[pl.ds(b, 1), dm:2 * dm]) + mod_ref[pl.ds(b, 1), 0:dm]).astype(BF16)

        def group(kk):
            z_scr[rows, kk * gw:(kk + 1) * gw] = _dot(h, win_ref[kk])

        def save_residual():
            xres_scr[rows, :] = DN_ALPHA * x

        return [functools.partial(group, kk) for kk in range(n_groups)], save_residual

    def mix_stage(r0, states):
        bp = jnp.maximum(i - 1, 0) // tps
        rows = slice(r0, r0 + SUB_ROWS)
        st = {}

        def load_and_rotate():
            u0 = HALO + r0
            u_scr[u0:u0 + SUB_ROWS, :] = z_scr[rows, 0:wc] * z_scr[rows, wc:2 * wc]
            conv = (convw_ref[0, 0:1, :] * u_scr[u0 - 2:u0 - 2 + SUB_ROWS, :]
                    + convw_ref[0, 1:2, :] * u_scr[u0 - 1:u0 - 1 + SUB_ROWS, :]
                    + convw_ref[0, 2:3, :] * u_scr[u0:u0 + SUB_ROWS, :])
            st["y_conv"] = (z_scr[rows, 2 * wc:3 * wc] * conv).astype(BF16)
            o = 3 * wc
            for c in range(SUB_ROWS // L):
                crow = slice(r0 + c * L, r0 + (c + 1) * L)
                cq, sq = cosq_ref[crow, :], sinq_ref[crow, :]
                ck, sk = cosk_ref[crow, :], sink_ref[crow, :]
                for hh in range(RET_HEADS):
                    qf = z_scr[crow, o + hh * hd:o + (hh + 1) * hd]
                    kf = z_scr[crow, o + wr + hh * hd:o + wr + (hh + 1) * hd]
                    vb = z_scr[crow, o + 2 * wr + hh * hd:o + 2 * wr + (hh + 1) * hd].astype(BF16)
                    qf = qf * cq + pltpu.roll(qf, hd // 2, 1) * sq
                    kf = kf * ck + pltpu.roll(kf, hd // 2, 1) * sk
                    st[c, hh] = dict(qb=qf.astype(BF16), kb=kf.astype(BF16),
                                     qx=(qf * xi_ref[hh]).astype(BF16),
                                     kz=(kf * zeta_ref[hh]).astype(BF16), vb=vb)

        def scores_and_kv():
            for c in range(SUB_ROWS // L):
                for hh in range(RET_HEADS):
                    e = st[c, hh]
                    e["s"] = lax.dot_general(e["qb"], e["kb"], (((1,), (1,)), ((), ())),
                                             preferred_element_type=F32)
                    e["kv"] = lax.dot_general(e["kz"], e["vb"], (((0,), (0,)), ((), ())),
                                              preferred_element_type=F32)

        def retention_out():
            r_rows = []
            for c in range(SUB_ROWS // L):
                r_heads = []
                for hh in range(RET_HEADS):
                    e = st[c, hh]
                    p = (e["s"] * dmask_ref[hh]).astype(BF16)
                    out = _dot(jnp.concatenate([p, e["qx"]], axis=1),
                               jnp.concatenate([e["vb"], states[hh].astype(BF16)], axis=0))
                    states[hh] = cdecay_ref[hh] * states[hh] + e["kv"]
                    mu = jnp.mean(out, axis=-1, keepdims=True)
                    dlt = out - mu
                    var = jnp.mean(dlt * dlt, axis=-1, keepdims=True)
                    r_heads.append(dlt * lax.rsqrt(var + LN_EPS))
                r_rows.append(jnp.concatenate(r_heads, axis=1))
            st["r"] = jnp.concatenate(r_rows, axis=0)

        def out_proj():
            gz = z_scr[rows, 3 * wc + 3 * wr:3 * wc + 4 * wr]
            y_ret = (jax.nn.silu(gz) * st["r"]).astype(BF16)
            st["mix"] = _dot(jnp.concatenate([st["y_conv"], y_ret], axis=1), wout_ref[...])

        def norm_store():
            gate = mod_ref[pl.ds(bp, 1), 2 * dm:3 * dm]
            o_ref[0, rows, :] = _layernorm(xres_scr[rows, :] + gate * st["mix"], g_ref[...], b_ref[...])

        return load_and_rotate, scores_and_kv, retention_out, out_proj, norm_store

    def mix_prologue():
        jp = jnp.maximum(i - 1, 0) % tps

        @pl.when(jp == 0)
        def _():
            state_scr[...] = jnp.zeros_like(state_scr)
            u_scr[0:HALO, :] = jnp.zeros((HALO, wc), F32)

        return [state_scr[hh] for hh in range(RET_HEADS)]

    def mix_epilogue(states):
        for hh in range(RET_HEADS):
            state_scr[hh] = states[hh]
        u_scr[0:HALO, :] = u_scr[ts:ts + HALO, :]

    @pl.when(i == 0)
    def _():
        for kk in range(n_groups):
            win_copy(kk).start()
        wout_copy().start()
        for r0 in range(0, ts, SUB_ROWS):
            groups, save_residual = in_proj_stage(r0)
            for kk, g in enumerate(groups):
                if r0 == 0 and kk in batches:
                    for cc in range(kk, batches[kk]):
                        win_copy(cc).wait()
                g()
            save_residual()
        wout_copy().wait()

    @pl.when(i == n_tiles)
    def _():
        states = mix_prologue()
        for r0 in range(0, ts, SUB_ROWS):
            for stage in mix_stage(r0, states):
                stage()
        mix_epilogue(states)

    @pl.when((i > 0) & (i < n_tiles))
    def _():
        states = mix_prologue()
        for r0 in range(0, ts, SUB_ROWS):
            groups, save_residual = in_proj_stage(r0)
            load_and_rotate, scores_and_kv, retention_out, out_proj, norm_store = mix_stage(r0, states)
            load_and_rotate()
            groups[0]()
            groups[1]()
            scores_and_kv()
            groups[2]()
            groups[3]()
            retention_out()
            groups[4]()
            groups[5]()
            out_proj()
            groups[6]()
            norm_store()
            save_residual()
        mix_epilogue(states)


def _mixer(x, mod, w_in, conv_w, w_out, ln_g, ln_b, tables, consts, w_gate, w_up, w_down):
    bsz, seq, d = x.shape
    ts = SEQ_TILE
    tps = seq // ts
    n_tiles = bsz * tps
    cosq, sinq, cosk, sink = tables
    dmask, xi_b, zeta_b, cdecay = consts
    hd = RET_HEAD_DIM
    cur = lambda i: jnp.minimum(i, n_tiles - 1)
    prev = lambda i: jnp.maximum(i - 1, 0)
    tab_spec = pl.BlockSpec((ts, hd), lambda i: (prev(i) % tps, 0))
    gu_rows = d // n_tiles
    dn_rows = D_FF // (n_tiles // 2)
    assert gu_rows * n_tiles == d and dn_rows * (n_tiles // 2) == D_FF
    assert gu_rows % BF16_SUBLANES == 0 and dn_rows % BF16_SUBLANES == 0
    nch = D_FF // FF_CHUNK
    gu_spec = pl.BlockSpec((gu_rows, D_FF), lambda i: (cur(i), 0))
    gu_out_spec = pl.BlockSpec((nch, gu_rows, FF_CHUNK), lambda i: (0, cur(i), 0))
    dn_spec = pl.BlockSpec((dn_rows, d), lambda i: (cur(i) // 2, 0))
    n_groups = N_IN_COLS // IN_GROUP
    hbm_spec = pl.BlockSpec(memory_space=pl.ANY)
    return pl.pallas_call(
        functools.partial(_mixer_kernel, n_tiles, tps),
        grid=(n_tiles + 1,),
        in_specs=[
            pl.BlockSpec((1, ts, d), lambda i: (cur(i) // tps, cur(i) % tps, 0)),
            _const_spec((bsz, N_MOD * d)),
            hbm_spec,
            _const_spec((1, CONV_K, CONV_WIDTH)),
            hbm_spec,
            _const_spec((1, d)),
            _const_spec((1, d)),
            tab_spec, tab_spec, tab_spec, tab_spec,
            _const_spec((RET_HEADS, RET_CHUNK, RET_CHUNK)),
            _const_spec((RET_HEADS, RET_CHUNK, hd)),
            _const_spec((RET_HEADS, RET_CHUNK, hd)),
            pl.BlockSpec(memory_space=pltpu.SMEM),
            gu_spec, gu_spec, dn_spec,
        ],
        out_specs=[
            pl.BlockSpec((1, ts, d), lambda i: (prev(i) // tps, prev(i) % tps, 0)),
            gu_out_spec, gu_out_spec, dn_spec,
        ],
        out_shape=[
            jax.ShapeDtypeStruct((bsz, seq, d), F32),
            jax.ShapeDtypeStruct((nch, d, FF_CHUNK), BF16),
            jax.ShapeDtypeStruct((nch, d, FF_CHUNK), BF16),
            jax.ShapeDtypeStruct((D_FF, d), BF16),
        ],
        scratch_shapes=[
            pltpu.VMEM((ts, N_IN_COLS), F32),
            pltpu.VMEM((ts, d), F32),
            pltpu.VMEM((ts + HALO, CONV_WIDTH), F32),
            pltpu.VMEM((RET_HEADS, hd, hd), F32),
            pltpu.VMEM((n_groups, d, IN_GROUP), BF16),
            pltpu.VMEM((d, d), BF16),
            pltpu.SemaphoreType.DMA((n_groups + 1,)),
        ],
        compiler_params=pltpu.CompilerParams(
            dimension_semantics=("arbitrary",),
            vmem_limit_bytes=VMEM_LIMIT_BYTES),
        name="mixer_ln1",
    )(x, mod, w_in, conv_w, w_out, ln_g, ln_b, cosq, sinq, cosk, sink,
      dmask, xi_b, zeta_b, cdecay, w_gate, w_up, w_down)


def _ffn_kernel(n_tiles, tiles_per_seq, x_ref, mod_ref, wg_hbm, wu_hbm, wd_hbm, g_ref, b_ref,
                o_ref, act_scr, xres_scr, wg_ref, wu_ref, wd_ref, wsem):
    tm = x_ref.shape[0]
    i = pl.program_id(0)
    dm = D_MODEL
    fc = FF_CHUNK
    nch = D_FF // fc
    batches = _wait_batches(nch)

    def gate_copy(c):
        return pltpu.make_async_copy(wg_hbm.at[c], wg_ref.at[c], wsem.at[0, c])

    def up_copy(c):
        return pltpu.make_async_copy(wu_hbm.at[c], wu_ref.at[c], wsem.at[1, c])

    def down_copy(c):
        rows = pl.ds(c * fc, fc)
        return pltpu.make_async_copy(wd_hbm.at[rows], wd_ref.at[rows], wsem.at[2, c])

    def down_and_norm():
        bp = jnp.maximum(i - 1, 0) // tiles_per_seq
        gate = mod_ref[pl.ds(bp, 1), 5 * dm:6 * dm]
        for r0 in range(0, tm, FFN_SUB_ROWS):
            rows = slice(r0, r0 + FFN_SUB_ROWS)
            ff = _dot(act_scr[rows, :], wd_ref[...])
            o_ref[rows, :] = _layernorm(xres_scr[rows, :] + gate * ff, g_ref[...], b_ref[...])

    def gate_up(wait_weights=False):
        b = jnp.minimum(i, n_tiles - 1) // tiles_per_seq
        shift = mod_ref[pl.ds(b, 1), 3 * dm:4 * dm]
        scale = mod_ref[pl.ds(b, 1), 4 * dm:5 * dm]
        for r0 in range(0, tm, FFN_SUB_ROWS):
            rows = slice(r0, r0 + FFN_SUB_ROWS)
            x = x_ref[rows, :]
            h = (x * (1.0 + scale) + shift).astype(BF16)
            for c in range(nch):
                if wait_weights and r0 == 0 and c in batches:
                    for cc in range(c, batches[c]):
                        gate_copy(cc).wait()
                        up_copy(cc).wait()
                act_scr[rows, c * fc:(c + 1) * fc] = (jax.nn.silu(_dot(h, wg_ref[c]))
                                                      * _dot(h, wu_ref[c])).astype(BF16)
            xres_scr[rows, :] = DN_ALPHA * x

    @pl.when(i == 0)
    def _():
        for c in range(nch):
            gate_copy(c).start()
            up_copy(c).start()
        for c in range(nch):
            down_copy(c).start()
        gate_up(wait_weights=True)
        for c in range(nch):
            down_copy(c).wait()

    @pl.when(i == n_tiles)
    def _():
        down_and_norm()

    @pl.when((i > 0) & (i < n_tiles))
    def _():
        down_and_norm()
        gate_up()


def _ffn(x1, mod, w_gate, w_up, w_down, ln_g, ln_b, seq):
    t, d = x1.shape
    bsz = t // seq
    tm = FFN_TILE
    n_tiles = t // tm
    nch = D_FF // FF_CHUNK
    hbm_spec = pl.BlockSpec(memory_space=pl.ANY)
    return pl.pallas_call(
        functools.partial(_ffn_kernel, n_tiles, seq // tm),
        grid=(n_tiles + 1,),
        in_specs=[
            pl.BlockSpec((tm, d), lambda i: (jnp.minimum(i, n_tiles - 1), 0)),
            _const_spec((bsz, N_MOD * d)),
            hbm_spec, hbm_spec, hbm_spec,
            _const_spec((1, d)),
            _const_spec((1, d)),
        ],
        out_specs=pl.BlockSpec((tm, d), lambda i: (jnp.maximum(i - 1, 0), 0)),
        out_shape=jax.ShapeDtypeStruct((t, d), F32),
        scratch_shapes=[
            pltpu.VMEM((tm, D_FF), BF16),
            pltpu.VMEM((tm, d), F32),
            pltpu.VMEM((nch, d, FF_CHUNK), BF16),
            pltpu.VMEM((nch, d, FF_CHUNK), BF16),
            pltpu.VMEM((D_FF, d), BF16),
            pltpu.SemaphoreType.DMA((3, nch)),
        ],
        compiler_params=pltpu.CompilerParams(
            dimension_semantics=("arbitrary",),
            vmem_limit_bytes=VMEM_LIMIT_BYTES),
        name="ffn_ln2",
    )(x1, mod, w_gate, w_up, w_down, ln_g, ln_b)


def _rotary_tables(seq):
    d = RET_HEAD_DIM
    inv_freq = ROPE_BASE ** (-np.arange(0, d, 2, dtype=np.float64) / d)
    ang = np.arange(seq, dtype=np.float64)[:, None] * inv_freq[None, :]
    cos, sin = np.cos(ang), np.sin(ang)
    cosq = np.concatenate([cos, cos], axis=-1)
    sinq = np.concatenate([-sin, sin], axis=-1)
    kscale = d ** -0.5
    return tuple(jnp.asarray(t.astype(np.float32)) for t in (cosq, sinq, cosq * kscale, sinq * kscale))


def _retention_consts():
    h, L, d = RET_HEADS, RET_CHUNK, RET_HEAD_DIM
    log_g = np.log1p(-(2.0 ** (-5.0 - np.arange(h, dtype=np.float64))))
    idx = np.arange(L, dtype=np.float64)
    diff = idx[:, None] - idx[None, :]
    dmask = np.where(diff >= 0, np.exp(log_g[:, None, None] * np.maximum(diff, 0.0)), 0.0)
    zeta = np.exp(log_g[:, None] * (L - 1 - idx)[None, :])
    xi = np.exp(log_g[:, None] * (idx + 1)[None, :])
    cdecay = np.exp(log_g * L)
    bcast = lambda a: np.broadcast_to(a[:, :, None], (h, L, d))
    return tuple(jnp.asarray(np.ascontiguousarray(t).astype(np.float32))
                 for t in (dmask, bcast(xi), bcast(zeta), cdecay))


def kernel(x, c, ada_w, ada_b, w_in, conv_w, w_out, ln1_g, ln1_b, w_gate, w_up, w_down, ln2_g, ln2_b):
    bsz, seq, d = x.shape
    assert ada_w.shape[0] == DEPTH
    assert seq % SEQ_TILE == 0 and seq % FFN_TILE == 0 and D_FF % FF_CHUNK == 0 and N_IN_COLS % IN_GROUP == 0
    assert SEQ_TILE % SUB_ROWS == 0 and FFN_TILE % FFN_SUB_ROWS == 0 and SUB_ROWS % RET_CHUNK == 0
    tables = _rotary_tables(seq)
    consts = _retention_consts()
    for layer in range(DEPTH):
        mod, win_bf, wout_bf = _adaln(c, ada_w[layer], ada_b[layer], w_in[layer], w_out[layer])
        x1, wg_bf, wu_bf, wd_bf = _mixer(
            x, mod, win_bf, conv_w, wout_bf,
            ln1_g[layer].reshape(1, d), ln1_b[layer].reshape(1, d), tables, consts,
            w_gate[layer], w_up[layer], w_down[layer])
        x2 = _ffn(x1.reshape(bsz * seq, d), mod, wg_bf, wu_bf, wd_bf,
                  ln2_g[layer].reshape(1, d), ln2_b[layer].reshape(1, d), seq)
        x = x2.reshape(bsz, seq, d)
    return x
```

```python
import functools

import jax
import jax.numpy as jnp
import numpy as np
from jax import lax
from jax.experimental import pallas as pl
from jax.experimental.pallas import tpu as pltpu

D_MODEL = 1024
CONV_WIDTH = 512
CONV_K = 3
RET_WIDTH = 512
RET_HEADS = 4
RET_HEAD_DIM = 128
RET_CHUNK = 128
ROPE_BASE = 10000.0
N_IN_COLS = 3 * CONV_WIDTH + 4 * RET_WIDTH
D_FF = 2816
DEPTH = 1
DN_ALPHA = float((2 * DEPTH) ** 0.25)
LN_EPS = 1e-5
N_MOD = 6

SEQ_TILE = 512
FFN_TILE = 512
SUB_ROWS = 256
FFN_SUB_ROWS = 256
FF_CHUNK = 256
IN_GROUP = 512
WAIT_STARTS = (0, 1, 3, 6)
HALO = 8
BF16_SUBLANES = 16
LANES = 128
ADALN_STEPS = 4
W_SLOTS = 3
VMEM_LIMIT_BYTES = 56 * 1024 * 1024

F32 = jnp.float32
BF16 = jnp.bfloat16


def _layernorm(v, g, b):
    mu = jnp.mean(v, axis=-1, keepdims=True)
    d = v - mu
    var = jnp.mean(d * d, axis=-1, keepdims=True)
    return d * lax.rsqrt(var + LN_EPS) * g + b


def _dot(a, b):
    return jnp.dot(a, b, preferred_element_type=F32)


def _wait_batches(n):
    starts = [k for k in WAIT_STARTS if k < n]
    return {k: e for k, e in zip(starts, starts[1:] + [n])}


def _const_spec(shape):
    zeros = (0,) * len(shape)
    return pl.BlockSpec(shape, lambda i: zeros, pipeline_mode=pl.Buffered(1))


def _adaln_kernel(c_ref, w_ref, b_ref, o_ref):
    sc = jax.nn.silu(c_ref[...])
    o_ref[...] = _dot(sc.astype(BF16), w_ref[...].astype(BF16)) + b_ref[...]


def _adaln(c, ada_w, ada_b):
    bsz, d = c.shape
    n = N_MOD * d
    steps = ADALN_STEPS
    tn = n // steps
    assert tn * steps == n and tn % LANES == 0
    return pl.pallas_call(
        _adaln_kernel,
        grid=(steps,),
        in_specs=[
            pl.BlockSpec((bsz, d), lambda j: (0, 0)),
            pl.BlockSpec((d, tn), lambda j: (0, j)),
            pl.BlockSpec((1, tn), lambda j: (0, j)),
        ],
        out_specs=pl.BlockSpec((bsz, tn), lambda j: (0, j)),
        out_shape=jax.ShapeDtypeStruct((bsz, n), F32),
        compiler_params=pltpu.CompilerParams(
            dimension_semantics=("arbitrary",),
            vmem_limit_bytes=VMEM_LIMIT_BYTES),
        name="adaln_mod",
    )(c, ada_w, ada_b.reshape(1, n))


def _mixer_kernel(n_tiles, tps, x_ref, mod_ref, win_hbm, convw_ref, wout_hbm, g_ref, b_ref,
                  cosq_ref, sinq_ref, cosk_ref, sink_ref,
                  dmask_ref, xi_ref, zeta_ref, cdecay_ref,
                  wg_ref, wu_ref, wd_ref,
                  o_ref, wg_bf_ref, wu_bf_ref, wd_bf_ref,
                  z_scr, xres_scr, u_scr, state_scr, win_ref, wout_ref, stage_scr, wsem):
    for c in range(D_FF // FF_CHUNK):
        wg_bf_ref[c] = wg_ref[:, c * FF_CHUNK:(c + 1) * FF_CHUNK].astype(BF16)
        wu_bf_ref[c] = wu_ref[:, c * FF_CHUNK:(c + 1) * FF_CHUNK].astype(BF16)
    wd_bf_ref[...] = wd_ref[...].astype(BF16)

    ts = x_ref.shape[1]
    wc, wr, hd, L = CONV_WIDTH, RET_WIDTH, RET_HEAD_DIM, RET_CHUNK
    dm = D_MODEL
    gw = IN_GROUP
    n_groups = N_IN_COLS // gw
    i = pl.program_id(0)
    n_pieces = n_groups + dm // gw

    def piece_copy(p):
        if p < n_groups:
            src = win_hbm.at[:, pl.ds(p * gw, gw)]
        else:
            src = wout_hbm.at[:, pl.ds((p - n_groups) * gw, gw)]
        return pltpu.make_async_copy(src, stage_scr.at[p % W_SLOTS], wsem.at[p % W_SLOTS])

    def in_proj_stage(r0):
        b = jnp.minimum(i, n_tiles - 1) // tps
        rows = slice(r0, r0 + SUB_ROWS)
        x = x_ref[0, rows, :]
        h = (x * (1.0 + mod_ref[pl.ds(b, 1), dm:2 * dm]) + mod_ref[pl.ds(b, 1), 0:dm]).astype(BF16)

        def group(kk):
            z_scr[rows, kk * gw:(kk + 1) * gw] = _dot(h, win_ref[kk])

        def save_residual():
            xres_scr[rows, :] = DN_ALPHA * x

        return [functools.partial(group, kk) for kk in range(n_groups)], save_residual

    def mix_stage(r0, states):
        bp = jnp.maximum(i - 1, 0) // tps
        rows = slice(r0, r0 + SUB_ROWS)
        st = {}

        def load_and_rotate():
            u0 = HALO + r0
            u_scr[u0:u0 + SUB_ROWS, :] = z_scr[rows, 0:wc] * z_scr[rows, wc:2 * wc]
            conv = (convw_ref[0, 0:1, :] * u_scr[u0 - 2:u0 - 2 + SUB_ROWS, :]
                    + convw_ref[0, 1:2, :] * u_scr[u0 - 1:u0 - 1 + SUB_ROWS, :]
                    + convw_ref[0, 2:3, :] * u_scr[u0:u0 + SUB_ROWS, :])
            st["y_conv"] = (z_scr[rows, 2 * wc:3 * wc] * conv).astype(BF16)
            o = 3 * wc
            for c in range(SUB_ROWS // L):
                crow = slice(r0 + c * L, r0 + (c + 1) * L)
                cq, sq = cosq_ref[crow, :], sinq_ref[crow, :]
                ck, sk = cosk_ref[crow, :], sink_ref[crow, :]
                for hh in range(RET_HEADS):
                    qf = z_scr[crow, o + hh * hd:o + (hh + 1) * hd]
                    kf = z_scr[crow, o + wr + hh * hd:o + wr + (hh + 1) * hd]
                    vb = z_scr[crow, o + 2 * wr + hh * hd:o + 2 * wr + (hh + 1) * hd].astype(BF16)
                    qf = qf * cq + pltpu.roll(qf, hd // 2, 1) * sq
                    kf = kf * ck + pltpu.roll(kf, hd // 2, 1) * sk
                    st[c, hh] = dict(qb=qf.astype(BF16), kb=kf.astype(BF16),
                                     qx=(qf * xi_ref[hh]).astype(BF16),
                                     kz=(kf * zeta_ref[hh]).astype(BF16), vb=vb)

        def scores_and_kv():
            for c in range(SUB_ROWS // L):
                for hh in range(RET_HEADS):
                    e = st[c, hh]
                    e["s"] = lax.dot_general(e["qb"], e["kb"], (((1,), (1,)), ((), ())),
                                             preferred_element_type=F32)
                    e["kv"] = lax.dot_general(e["kz"], e["vb"], (((0,), (0,)), ((), ())),
                                              preferred_element_type=F32)

        def retention_out():
            r_rows = []
            for c in range(SUB_ROWS // L):
                r_heads = []
                for hh in range(RET_HEADS):
                    e = st[c, hh]
                    p = (e["s"] * dmask_ref[hh]).astype(BF16)
                    out = _dot(jnp.concatenate([p, e["qx"]], axis=1),
                               jnp.concatenate([e["vb"], states[hh].astype(BF16)], axis=0))
                    states[hh] = cdecay_ref[hh] * states[hh] + e["kv"]
                    mu = jnp.mean(out, axis=-1, keepdims=True)
                    dlt = out - mu
                    var = jnp.mean(dlt * dlt, axis=-1, keepdims=True)
                    r_heads.append(dlt * lax.rsqrt(var + LN_EPS))
                r_rows.append(jnp.concatenate(r_heads, axis=1))
            st["r"] = jnp.concatenate(r_rows, axis=0)

        def out_proj():
            gz = z_scr[rows, 3 * wc + 3 * wr:3 * wc + 4 * wr]
            y_ret = (jax.nn.silu(gz) * st["r"]).astype(BF16)
            st["mix"] = _dot(jnp.concatenate([st["y_conv"], y_ret], axis=1), wout_ref[...])

        def norm_store():
            gate = mod_ref[pl.ds(bp, 1), 2 * dm:3 * dm]
            o_ref[0, rows, :] = _layernorm(xres_scr[rows, :] + gate * st["mix"], g_ref[...], b_ref[...])

        return load_and_rotate, scores_and_kv, retention_out, out_proj, norm_store

    def mix_prologue():
        jp = jnp.maximum(i - 1, 0) % tps

        @pl.when(jp == 0)
        def _():
            state_scr[...] = jnp.zeros_like(state_scr)
            u_scr[0:HALO, :] = jnp.zeros((HALO, wc), F32)

        return [state_scr[hh] for hh in range(RET_HEADS)]

    def mix_epilogue(states):
        for hh in range(RET_HEADS):
            state_scr[hh] = states[hh]
        u_scr[0:HALO, :] = u_scr[ts:ts + HALO, :]

    @pl.when(i == 0)
    def _():
        stages = [in_proj_stage(r0) for r0 in range(0, ts, SUB_ROWS)]
        for p in range(min(W_SLOTS, n_pieces)):
            piece_copy(p).start()
        for p in range(n_pieces):
            piece_copy(p).wait()
            piece = stage_scr[p % W_SLOTS].astype(BF16)
            if p < n_groups:
                win_ref[p] = piece
            else:
                wout_ref[:, (p - n_groups) * gw:(p - n_groups + 1) * gw] = piece
            if p + W_SLOTS < n_pieces:
                piece_copy(p + W_SLOTS).start()
            if p < n_groups:
                for groups, _ in stages:
                    groups[p]()
        for _, save_residual in stages:
            save_residual()

    @pl.when(i == n_tiles)
    def _():
        states = mix_prologue()
        for r0 in range(0, ts, SUB_ROWS):
            for stage in mix_stage(r0, states):
                stage()
        mix_epilogue(states)

    @pl.when((i > 0) & (i < n_tiles))
    def _():
        states = mix_prologue()
        for r0 in range(0, ts, SUB_ROWS):
            groups, save_residual = in_proj_stage(r0)
            load_and_rotate, scores_and_kv, retention_out, out_proj, norm_store = mix_stage(r0, states)
            load_and_rotate()
            groups[0]()
            groups[1]()
            scores_and_kv()
            groups[2]()
            groups[3]()
            retention_out()
            groups[4]()
            groups[5]()
            out_proj()
            groups[6]()
            norm_store()
            save_residual()
        mix_epilogue(states)


def _mixer(x, mod, w_in, conv_w, w_out, ln_g, ln_b, tables, consts, w_gate, w_up, w_down):
    bsz, seq, d = x.shape
    ts = SEQ_TILE
    tps = seq // ts
    n_tiles = bsz * tps
    cosq, sinq, cosk, sink = tables
    dmask, xi_b, zeta_b, cdecay = consts
    hd = RET_HEAD_DIM
    cur = lambda i: jnp.minimum(i, n_tiles - 1)
    prev = lambda i: jnp.maximum(i - 1, 0)
    tab_spec = pl.BlockSpec((ts, hd), lambda i: (prev(i) % tps, 0))
    gu_rows = d // n_tiles
    dn_rows = D_FF // (n_tiles // 2)
    assert gu_rows * n_tiles == d and dn_rows * (n_tiles // 2) == D_FF
    assert gu_rows % BF16_SUBLANES == 0 and dn_rows % BF16_SUBLANES == 0
    nch = D_FF // FF_CHUNK
    gu_spec = pl.BlockSpec((gu_rows, D_FF), lambda i: (cur(i), 0))
    gu_out_spec = pl.BlockSpec((nch, gu_rows, FF_CHUNK), lambda i: (0, cur(i), 0))
    dn_spec = pl.BlockSpec((dn_rows, d), lambda i: (cur(i) // 2, 0))
    n_groups = N_IN_COLS // IN_GROUP
    hbm_spec = pl.BlockSpec(memory_space=pl.ANY)
    return pl.pallas_call(
        functools.partial(_mixer_kernel, n_tiles, tps),
        grid=(n_tiles + 1,),
        in_specs=[
            pl.BlockSpec((1, ts, d), lambda i: (cur(i) // tps, cur(i) % tps, 0)),
            _const_spec((bsz, N_MOD * d)),
            hbm_spec,
            _const_spec((1, CONV_K, CONV_WIDTH)),
            hbm_spec,
            _const_spec((1, d)),
            _const_spec((1, d)),
            tab_spec, tab_spec, tab_spec, tab_spec,
            _const_spec((RET_HEADS, RET_CHUNK, RET_CHUNK)),
            _const_spec((RET_HEADS, RET_CHUNK, hd)),
            _const_spec((RET_HEADS, RET_CHUNK, hd)),
            pl.BlockSpec(memory_space=pltpu.SMEM),
            gu_spec, gu_spec, dn_spec,
        ],
        out_specs=[
            pl.BlockSpec((1, ts, d), lambda i: (prev(i) // tps, prev(i) % tps, 0)),
            gu_out_spec, gu_out_spec, dn_spec,
        ],
        out_shape=[
            jax.ShapeDtypeStruct((bsz, seq, d), F32),
            jax.ShapeDtypeStruct((nch, d, FF_CHUNK), BF16),
            jax.ShapeDtypeStruct((nch, d, FF_CHUNK), BF16),
            jax.ShapeDtypeStruct((D_FF, d), BF16),
        ],
        scratch_shapes=[
            pltpu.VMEM((ts, N_IN_COLS), F32),
            pltpu.VMEM((ts, d), F32),
            pltpu.VMEM((ts + HALO, CONV_WIDTH), F32),
            pltpu.VMEM((RET_HEADS, hd, hd), F32),
            pltpu.VMEM((n_groups, d, IN_GROUP), BF16),
            pltpu.VMEM((d, d), BF16),
            pltpu.VMEM((W_SLOTS, d, IN_GROUP), F32),
            pltpu.SemaphoreType.DMA((W_SLOTS,)),
        ],
        compiler_params=pltpu.CompilerParams(
            dimension_semantics=("arbitrary",),
            vmem_limit_bytes=VMEM_LIMIT_BYTES),
        name="mixer_ln1",
    )(x, mod, w_in, conv_w, w_out, ln_g, ln_b, cosq, sinq, cosk, sink,
      dmask, xi_b, zeta_b, cdecay, w_gate, w_up, w_down)


def _ffn_kernel(n_tiles, tiles_per_seq, x_ref, mod_ref, wg_hbm, wu_hbm, wd_hbm, g_ref, b_ref,
                o_ref, act_scr, xres_scr, wg_ref, wu_ref, wd_ref, wsem):
    tm = x_ref.shape[0]
    i = pl.program_id(0)
    dm = D_MODEL
    fc = FF_CHUNK
    nch = D_FF // fc
    batches = _wait_batches(nch)

    def gate_copy(c):
        return pltpu.make_async_copy(wg_hbm.at[c], wg_ref.at[c], wsem.at[0, c])

    def up_copy(c):
        return pltpu.make_async_copy(wu_hbm.at[c], wu_ref.at[c], wsem.at[1, c])

    def down_copy(c):
        rows = pl.ds(c * fc, fc)
        return pltpu.make_async_copy(wd_hbm.at[rows], wd_ref.at[rows], wsem.at[2, c])

    def down_and_norm():
        bp = jnp.maximum(i - 1, 0) // tiles_per_seq
        gate = mod_ref[pl.ds(bp, 1), 5 * dm:6 * dm]
        for r0 in range(0, tm, FFN_SUB_ROWS):
            rows = slice(r0, r0 + FFN_SUB_ROWS)
            ff = _dot(act_scr[rows, :], wd_ref[...])
            o_ref[rows, :] = _layernorm(xres_scr[rows, :] + gate * ff, g_ref[...], b_ref[...])

    def gate_up(wait_weights=False):
        b = jnp.minimum(i, n_tiles - 1) // tiles_per_seq
        shift = mod_ref[pl.ds(b, 1), 3 * dm:4 * dm]
        scale = mod_ref[pl.ds(b, 1), 4 * dm:5 * dm]
        for r0 in range(0, tm, FFN_SUB_ROWS):
            rows = slice(r0, r0 + FFN_SUB_ROWS)
            x = x_ref[rows, :]
            h = (x * (1.0 + scale) + shift).astype(BF16)
            for c in range(nch):
                if wait_weights and r0 == 0 and c in batches:
                    for cc in range(c, batches[c]):
                        gate_copy(cc).wait()
                        up_copy(cc).wait()
                act_scr[rows, c * fc:(c + 1) * fc] = (jax.nn.silu(_dot(h, wg_ref[c]))
                                                      * _dot(h, wu_ref[c])).astype(BF16)
            xres_scr[rows, :] = DN_ALPHA * x

    @pl.when(i == 0)
    def _():
        for c in range(nch):
            gate_copy(c).start()
            up_copy(c).start()
        for c in range(nch):
            down_copy(c).start()
        gate_up(wait_weights=True)
        for c in range(nch):
            down_copy(c).wait()

    @pl.when(i == n_tiles)
    def _():
        down_and_norm()

    @pl.when((i > 0) & (i < n_tiles))
    def _():
        down_and_norm()
        gate_up()


def _ffn(x1, mod, w_gate, w_up, w_down, ln_g, ln_b, seq):
    t, d = x1.shape
    bsz = t // seq
    tm = FFN_TILE
    n_tiles = t // tm
    nch = D_FF // FF_CHUNK
    hbm_spec = pl.BlockSpec(memory_space=pl.ANY)
    return pl.pallas_call(
        functools.partial(_ffn_kernel, n_tiles, seq // tm),
        grid=(n_tiles + 1,),
        in_specs=[
            pl.BlockSpec((tm, d), lambda i: (jnp.minimum(i, n_tiles - 1), 0)),
            _const_spec((bsz, N_MOD * d)),
            hbm_spec, hbm_spec, hbm_spec,
            _const_spec((1, d)),
            _const_spec((1, d)),
        ],
        out_specs=pl.BlockSpec((tm, d), lambda i: (jnp.maximum(i - 1, 0), 0)),
        out_shape=jax.ShapeDtypeStruct((t, d), F32),
        scratch_shapes=[
            pltpu.VMEM((tm, D_FF), BF16),
            pltpu.VMEM((tm, d), F32),
            pltpu.VMEM((nch, d, FF_CHUNK), BF16),
            pltpu.VMEM((nch, d, FF_CHUNK), BF16),
            pltpu.VMEM((D_FF, d), BF16),
            pltpu.SemaphoreType.DMA((3, nch)),
        ],
        compiler_params=pltpu.CompilerParams(
            dimension_semantics=("arbitrary",),
            vmem_limit_bytes=VMEM_LIMIT_BYTES),
        name="ffn_ln2",
    )(x1, mod, w_gate, w_up, w_down, ln_g, ln_b)


def _rotary_tables(seq):
    d = RET_HEAD_DIM
    inv_freq = ROPE_BASE ** (-np.arange(0, d, 2, dtype=np.float64) / d)
    ang = np.arange(seq, dtype=np.float64)[:, None] * inv_freq[None, :]
    cos, sin = np.cos(ang), np.sin(ang)
    cosq = np.concatenate([cos, cos], axis=-1)
    sinq = np.concatenate([-sin, sin], axis=-1)
    kscale = d ** -0.5
    return tuple(jnp.asarray(t.astype(np.float32)) for t in (cosq, sinq, cosq * kscale, sinq * kscale))


def _retention_consts():
    h, L, d = RET_HEADS, RET_CHUNK, RET_HEAD_DIM
    log_g = np.log1p(-(2.0 ** (-5.0 - np.arange(h, dtype=np.float64))))
    idx = np.arange(L, dtype=np.float64)
    diff = idx[:, None] - idx[None, :]
    dmask = np.where(diff >= 0, np.exp(log_g[:, None, None] * np.maximum(diff, 0.0)), 0.0)
    zeta = np.exp(log_g[:, None] * (L - 1 - idx)[None, :])
    xi = np.exp(log_g[:, None] * (idx + 1)[None, :])
    cdecay = np.exp(log_g * L)
    bcast = lambda a: np.broadcast_to(a[:, :, None], (h, L, d))
    return tuple(jnp.asarray(np.ascontiguousarray(t).astype(np.float32))
                 for t in (dmask, bcast(xi), bcast(zeta), cdecay))


def kernel(x, c, ada_w, ada_b, w_in, conv_w, w_out, ln1_g, ln1_b, w_gate, w_up, w_down, ln2_g, ln2_b):
    bsz, seq, d = x.shape
    assert ada_w.shape[0] == DEPTH
    assert seq % SEQ_TILE == 0 and seq % FFN_TILE == 0 and D_FF % FF_CHUNK == 0 and N_IN_COLS % IN_GROUP == 0
    assert SEQ_TILE % SUB_ROWS == 0 and FFN_TILE % FFN_SUB_ROWS == 0 and SUB_ROWS % RET_CHUNK == 0
    tables = _rotary_tables(seq)
    consts = _retention_consts()
    for layer in range(DEPTH):
        mod = _adaln(c, ada_w[layer], ada_b[layer])
        x1, wg_bf, wu_bf, wd_bf = _mixer(
            x, mod, w_in[layer], conv_w, w_out[layer],
            ln1_g[layer].reshape(1, d), ln1_b[layer].reshape(1, d), tables, consts,
            w_gate[layer], w_up[layer], w_down[layer])
        x2 = _ffn(x1.reshape(bsz * seq, d), mod, wg_bf, wu_bf, wd_bf,
                  ln2_g[layer].reshape(1, d), ln2_b[layer].reshape(1, d), seq)
        x = x2.reshape(bsz, seq, d)
    return x
```

```python
import functools

import jax
import jax.numpy as jnp
import numpy as np
from jax import lax
from jax.experimental import pallas as pl
from jax.experimental.pallas import tpu as pltpu

D_MODEL = 1024
CONV_WIDTH = 512
CONV_K = 3
RET_WIDTH = 512
RET_HEADS = 4
RET_HEAD_DIM = 128
RET_CHUNK = 128
ROPE_BASE = 10000.0
N_IN_COLS = 3 * CONV_WIDTH + 4 * RET_WIDTH
D_FF = 2816
DEPTH = 1
DN_ALPHA = float((2 * DEPTH) ** 0.25)
LN_EPS = 1e-5
N_MOD = 6

SEQ_TILE = 512
FFN_TILE = 512
SUB_ROWS = 256
FFN_SUB_ROWS = 256
FF_CHUNK = 256
IN_GROUP = 512
WAIT_STARTS = (0, 1, 3, 6)
HALO = 8
BF16_SUBLANES = 16
LANES = 128
ADALN_STEPS = 4
W_SLOTS = 3
VMEM_LIMIT_BYTES = 56 * 1024 * 1024

F32 = jnp.float32
BF16 = jnp.bfloat16


def _layernorm(v, g, b):
    mu = jnp.mean(v, axis=-1, keepdims=True)
    d = v - mu
    var = jnp.mean(d * d, axis=-1, keepdims=True)
    return d * lax.rsqrt(var + LN_EPS) * g + b


def _dot(a, b):
    return jnp.dot(a, b, preferred_element_type=F32)


def _wait_batches(n):
    starts = [k for k in WAIT_STARTS if k < n]
    return {k: e for k, e in zip(starts, starts[1:] + [n])}


def _const_spec(shape):
    zeros = (0,) * len(shape)
    return pl.BlockSpec(shape, lambda i: zeros, pipeline_mode=pl.Buffered(1))


def _adaln_kernel(c_ref, w_ref, b_ref, o_ref):
    sc = jax.nn.silu(c_ref[...])
    o_ref[...] = _dot(sc.astype(BF16), w_ref[...].astype(BF16)) + b_ref[...]


def _adaln(c, ada_w, ada_b):
    bsz, d = c.shape
    n = N_MOD * d
    steps = ADALN_STEPS
    tn = n // steps
    assert tn * steps == n and tn % LANES == 0
    return pl.pallas_call(
        _adaln_kernel,
        grid=(steps,),
        in_specs=[
            pl.BlockSpec((bsz, d), lambda j: (0, 0)),
            pl.BlockSpec((d, tn), lambda j: (0, j)),
            pl.BlockSpec((1, tn), lambda j: (0, j)),
        ],
        out_specs=pl.BlockSpec((bsz, tn), lambda j: (0, j)),
        out_shape=jax.ShapeDtypeStruct((bsz, n), F32),
        compiler_params=pltpu.CompilerParams(
            dimension_semantics=("arbitrary",),
            vmem_limit_bytes=VMEM_LIMIT_BYTES),
        name="adaln_mod",
    )(c, ada_w, ada_b.reshape(1, n))


def _mixer_kernel(n_tiles, tps, x_ref, mod_ref, win_hbm, convw_ref, wout_hbm, g_ref, b_ref,
                  rot_ref,
                  dmask_ref, xi_ref, zeta_ref, cdecay_ref,
                  wg_ref, wu_ref, wd_ref,
                  o_ref, wg_bf_ref, wu_bf_ref, wd_bf_ref,
                  z_scr, xres_scr, u_scr, state_scr, win_ref, wout_ref, stage_scr, wsem):
    i = pl.program_id(0)

    @pl.when(i < n_tiles // 2)
    def _():
        for c in range(D_FF // FF_CHUNK):
            wg_bf_ref[c] = wg_ref[:, c * FF_CHUNK:(c + 1) * FF_CHUNK].astype(BF16)

    @pl.when((i >= n_tiles // 2) & (i < n_tiles))
    def _():
        for c in range(D_FF // FF_CHUNK):
            wu_bf_ref[c] = wu_ref[:, c * FF_CHUNK:(c + 1) * FF_CHUNK].astype(BF16)

    wd_bf_ref[...] = wd_ref[...].astype(BF16)

    ts = x_ref.shape[1]
    wc, wr, hd, L = CONV_WIDTH, RET_WIDTH, RET_HEAD_DIM, RET_CHUNK
    dm = D_MODEL
    gw = IN_GROUP
    n_groups = N_IN_COLS // gw
    n_pieces = n_groups + dm // gw

    def piece_copy(p):
        if p < n_groups:
            src = win_hbm.at[:, pl.ds(p * gw, gw)]
        else:
            src = wout_hbm.at[:, pl.ds((p - n_groups) * gw, gw)]
        return pltpu.make_async_copy(src, stage_scr.at[p % W_SLOTS], wsem.at[p % W_SLOTS])

    def in_proj_stage(r0):
        b = jnp.minimum(i, n_tiles - 1) // tps
        rows = slice(r0, r0 + SUB_ROWS)
        x = x_ref[0, rows, :]
        h = (x * (1.0 + mod_ref[pl.ds(b, 1), dm:2 * dm]) + mod_ref[pl.ds(b, 1), 0:dm]).astype(BF16)

        def group(kk):
            z_scr[rows, kk * gw:(kk + 1) * gw] = _dot(h, win_ref[kk])

        def save_residual():
            xres_scr[rows, :] = DN_ALPHA * x

        return [functools.partial(group, kk) for kk in range(n_groups)], save_residual

    def mix_stage(r0, states):
        bp = jnp.maximum(i - 1, 0) // tps
        jp = jnp.maximum(i - 1, 0) % tps
        rows = slice(r0, r0 + SUB_ROWS)
        st = {}

        def load_and_rotate():
            u0 = HALO + r0
            u_scr[u0:u0 + SUB_ROWS, :] = z_scr[rows, 0:wc] * z_scr[rows, wc:2 * wc]
            conv = (convw_ref[0, 0:1, :] * u_scr[u0 - 2:u0 - 2 + SUB_ROWS, :]
                    + convw_ref[0, 1:2, :] * u_scr[u0 - 1:u0 - 1 + SUB_ROWS, :]
                    + convw_ref[0, 2:3, :] * u_scr[u0:u0 + SUB_ROWS, :])
            st["y_conv"] = (z_scr[rows, 2 * wc:3 * wc] * conv).astype(BF16)
            o = 3 * wc
            for c in range(SUB_ROWS // L):
                crow = slice(r0 + c * L, r0 + (c + 1) * L)
                prow = pl.ds(pl.multiple_of(jp * ts + r0 + c * L, L), L)
                cq, sq = rot_ref[prow, 0:hd], rot_ref[prow, hd:2 * hd]
                ck, sk = rot_ref[prow, 2 * hd:3 * hd], rot_ref[prow, 3 * hd:4 * hd]
                for hh in range(RET_HEADS):
                    qf = z_scr[crow, o + hh * hd:o + (hh + 1) * hd]
                    kf = z_scr[crow, o + wr + hh * hd:o + wr + (hh + 1) * hd]
                    vb = z_scr[crow, o + 2 * wr + hh * hd:o + 2 * wr + (hh + 1) * hd].astype(BF16)
                    qf = qf * cq + pltpu.roll(qf, hd // 2, 1) * sq
                    kf = kf * ck + pltpu.roll(kf, hd // 2, 1) * sk
                    st[c, hh] = dict(qb=qf.astype(BF16), kb=kf.astype(BF16),
                                     qx=(qf * xi_ref[hh]).astype(BF16),
                                     kz=(kf * zeta_ref[hh]).astype(BF16), vb=vb)

        def scores_and_kv():
            for c in range(SUB_ROWS // L):
                for hh in range(RET_HEADS):
                    e = st[c, hh]
                    e["s"] = lax.dot_general(e["qb"], e["kb"], (((1,), (1,)), ((), ())),
                                             preferred_element_type=F32)
                    e["kv"] = lax.dot_general(e["kz"], e["vb"], (((0,), (0,)), ((), ())),
                                              preferred_element_type=F32)

        def retention_out():
            r_rows = []
            for c in range(SUB_ROWS // L):
                r_heads = []
                for hh in range(RET_HEADS):
                    e = st[c, hh]
                    p = (e["s"] * dmask_ref[hh]).astype(BF16)
                    out = _dot(jnp.concatenate([p, e["qx"]], axis=1),
                               jnp.concatenate([e["vb"], states[hh].astype(BF16)], axis=0))
                    states[hh] = cdecay_ref[hh] * states[hh] + e["kv"]
                    mu = jnp.mean(out, axis=-1, keepdims=True)
                    dlt = out - mu
                    var = jnp.mean(dlt * dlt, axis=-1, keepdims=True)
                    r_heads.append(dlt * lax.rsqrt(var + LN_EPS))
                r_rows.append(jnp.concatenate(r_heads, axis=1))
            st["r"] = jnp.concatenate(r_rows, axis=0)

        def out_proj():
            gz = z_scr[rows, 3 * wc + 3 * wr:3 * wc + 4 * wr]
            y_ret = (jax.nn.silu(gz) * st["r"]).astype(BF16)
            st["mix"] = _dot(jnp.concatenate([st["y_conv"], y_ret], axis=1), wout_ref[...])

        def norm_store():
            gate = mod_ref[pl.ds(bp, 1), 2 * dm:3 * dm]
            o_ref[0, rows, :] = _layernorm(xres_scr[rows, :] + gate * st["mix"], g_ref[...], b_ref[...])

        return load_and_rotate, scores_and_kv, retention_out, out_proj, norm_store

    def mix_prologue():
        jp = jnp.maximum(i - 1, 0) % tps

        @pl.when(jp == 0)
        def _():
            state_scr[...] = jnp.zeros_like(state_scr)
            u_scr[0:HALO, :] = jnp.zeros((HALO, wc), F32)

        return [state_scr[hh] for hh in range(RET_HEADS)]

    def mix_epilogue(states):
        for hh in range(RET_HEADS):
            state_scr[hh] = states[hh]
        u_scr[0:HALO, :] = u_scr[ts:ts + HALO, :]

    @pl.when(i == 0)
    def _():
        stages = [in_proj_stage(r0) for r0 in range(0, ts, SUB_ROWS)]
        for p in range(min(W_SLOTS, n_pieces)):
            piece_copy(p).start()
        for p in range(n_pieces):
            piece_copy(p).wait()
            piece = stage_scr[p % W_SLOTS].astype(BF16)
            if p < n_groups:
                win_ref[p] = piece
            else:
                wout_ref[:, (p - n_groups) * gw:(p - n_groups + 1) * gw] = piece
            if p + W_SLOTS < n_pieces:
                piece_copy(p + W_SLOTS).start()
            if p < n_groups:
                for groups, _ in stages:
                    groups[p]()
        for _, save_residual in stages:
            save_residual()

    @pl.when(i == n_tiles)
    def _():
        states = mix_prologue()
        for r0 in range(0, ts, SUB_ROWS):
            for stage in mix_stage(r0, states):
                stage()
        mix_epilogue(states)

    @pl.when((i > 0) & (i < n_tiles))
    def _():
        states = mix_prologue()
        for r0 in range(0, ts, SUB_ROWS):
            groups, save_residual = in_proj_stage(r0)
            load_and_rotate, scores_and_kv, retention_out, out_proj, norm_store = mix_stage(r0, states)
            load_and_rotate()
            groups[0]()
            groups[1]()
            scores_and_kv()
            groups[2]()
            groups[3]()
            retention_out()
            groups[4]()
            groups[5]()
            out_proj()
            groups[6]()
            norm_store()
            save_residual()
        mix_epilogue(states)


def _mixer(x, mod, w_in, conv_w, w_out, ln_g, ln_b, tables, consts, w_gate, w_up, w_down):
    bsz, seq, d = x.shape
    ts = SEQ_TILE
    tps = seq // ts
    n_tiles = bsz * tps
    dmask, xi_b, zeta_b, cdecay = consts
    hd = RET_HEAD_DIM
    cur = lambda i: jnp.minimum(i, n_tiles - 1)
    prev = lambda i: jnp.maximum(i - 1, 0)
    half = n_tiles // 2
    gu_rows = d // half
    dn_rows = D_FF // half
    assert 2 * half == n_tiles and gu_rows * half == d and dn_rows * half == D_FF
    assert gu_rows % BF16_SUBLANES == 0 and dn_rows % BF16_SUBLANES == 0
    nch = D_FF // FF_CHUNK
    g_idx = lambda i: jnp.minimum(i, half - 1)
    u_idx = lambda i: jnp.clip(i - half, 0, half - 1)
    g_spec = pl.BlockSpec((gu_rows, D_FF), lambda i: (g_idx(i), 0))
    u_spec = pl.BlockSpec((gu_rows, D_FF), lambda i: (u_idx(i), 0))
    g_out_spec = pl.BlockSpec((nch, gu_rows, FF_CHUNK), lambda i: (0, g_idx(i), 0))
    u_out_spec = pl.BlockSpec((nch, gu_rows, FF_CHUNK), lambda i: (0, u_idx(i), 0))
    dn_spec = pl.BlockSpec((dn_rows, d), lambda i: (cur(i) // 2, 0))
    n_groups = N_IN_COLS // IN_GROUP
    hbm_spec = pl.BlockSpec(memory_space=pl.ANY)
    return pl.pallas_call(
        functools.partial(_mixer_kernel, n_tiles, tps),
        grid=(n_tiles + 1,),
        in_specs=[
            pl.BlockSpec((1, ts, d), lambda i: (cur(i) // tps, cur(i) % tps, 0)),
            _const_spec((bsz, N_MOD * d)),
            hbm_spec,
            _const_spec((1, CONV_K, CONV_WIDTH)),
            hbm_spec,
            _const_spec((1, d)),
            _const_spec((1, d)),
            _const_spec((seq, 4 * hd)),
            _const_spec((RET_HEADS, RET_CHUNK, RET_CHUNK)),
            _const_spec((RET_HEADS, RET_CHUNK, hd)),
            _const_spec((RET_HEADS, RET_CHUNK, hd)),
            pl.BlockSpec(memory_space=pltpu.SMEM),
            g_spec, u_spec, dn_spec,
        ],
        out_specs=[
            pl.BlockSpec((1, ts, d), lambda i: (prev(i) // tps, prev(i) % tps, 0)),
            g_out_spec, u_out_spec, dn_spec,
        ],
        out_shape=[
            jax.ShapeDtypeStruct((bsz, seq, d), F32),
            jax.ShapeDtypeStruct((nch, d, FF_CHUNK), BF16),
            jax.ShapeDtypeStruct((nch, d, FF_CHUNK), BF16),
            jax.ShapeDtypeStruct((D_FF, d), BF16),
        ],
        scratch_shapes=[
            pltpu.VMEM((ts, N_IN_COLS), F32),
            pltpu.VMEM((ts, d), F32),
            pltpu.VMEM((ts + HALO, CONV_WIDTH), F32),
            pltpu.VMEM((RET_HEADS, hd, hd), F32),
            pltpu.VMEM((n_groups, d, IN_GROUP), BF16),
            pltpu.VMEM((d, d), BF16),
            pltpu.VMEM((W_SLOTS, d, IN_GROUP), F32),
            pltpu.SemaphoreType.DMA((W_SLOTS,)),
        ],
        compiler_params=pltpu.CompilerParams(
            dimension_semantics=("arbitrary",),
            vmem_limit_bytes=VMEM_LIMIT_BYTES),
        name="mixer_ln1",
    )(x, mod, w_in, conv_w, w_out, ln_g, ln_b, tables,
      dmask, xi_b, zeta_b, cdecay, w_gate, w_up, w_down)


def _ffn_kernel(n_tiles, tiles_per_seq, x_ref, mod_ref, wg_hbm, wu_hbm, wd_hbm, g_ref, b_ref,
                o_ref, act_scr, xres_scr, wg_ref, wu_ref, wd_ref, wsem):
    tm = x_ref.shape[0]
    i = pl.program_id(0)
    dm = D_MODEL
    fc = FF_CHUNK
    nch = D_FF // fc
    batches = _wait_batches(nch)

    def gate_copy(c):
        return pltpu.make_async_copy(wg_hbm.at[c], wg_ref.at[c], wsem.at[0, c])

    def up_copy(c):
        return pltpu.make_async_copy(wu_hbm.at[c], wu_ref.at[c], wsem.at[1, c])

    def down_copy(c):
        rows = pl.ds(c * fc, fc)
        return pltpu.make_async_copy(wd_hbm.at[rows], wd_ref.at[rows], wsem.at[2, c])

    def down_and_norm():
        bp = jnp.maximum(i - 1, 0) // tiles_per_seq
        gate = mod_ref[pl.ds(bp, 1), 5 * dm:6 * dm]
        for r0 in range(0, tm, FFN_SUB_ROWS):
            rows = slice(r0, r0 + FFN_SUB_ROWS)
            ff = _dot(act_scr[rows, :], wd_ref[...])
            o_ref[rows, :] = _layernorm(xres_scr[rows, :] + gate * ff, g_ref[...], b_ref[...])

    def gate_up(wait_weights=False):
        b = jnp.minimum(i, n_tiles - 1) // tiles_per_seq
        shift = mod_ref[pl.ds(b, 1), 3 * dm:4 * dm]
        scale = mod_ref[pl.ds(b, 1), 4 * dm:5 * dm]
        for r0 in range(0, tm, FFN_SUB_ROWS):
            rows = slice(r0, r0 + FFN_SUB_ROWS)
            x = x_ref[rows, :]
            h = (x * (1.0 + scale) + shift).astype(BF16)
            for c in range(nch):
                if wait_weights and r0 == 0 and c in batches:
                    for cc in range(c, batches[c]):
                        gate_copy(cc).wait()
                        up_copy(cc).wait()
                act_scr[rows, c * fc:(c + 1) * fc] = (jax.nn.silu(_dot(h, wg_ref[c]))
                                                      * _dot(h, wu_ref[c])).astype(BF16)
            xres_scr[rows, :] = DN_ALPHA * x

    @pl.when(i == 0)
    def _():
        for c in range(nch):
            gate_copy(c).start()
            up_copy(c).start()
        for c in range(nch):
            down_copy(c).start()
        gate_up(wait_weights=True)
        for c in range(nch):
            down_copy(c).wait()

    @pl.when(i == n_tiles)
    def _():
        down_and_norm()

    @pl.when((i > 0) & (i < n_tiles))
    def _():
        down_and_norm()
        gate_up()


def _ffn(x1, mod, w_gate, w_up, w_down, ln_g, ln_b, seq):
    t, d = x1.shape
    bsz = t // seq
    tm = FFN_TILE
    n_tiles = t // tm
    nch = D_FF // FF_CHUNK
    hbm_spec = pl.BlockSpec(memory_space=pl.ANY)
    return pl.pallas_call(
        functools.partial(_ffn_kernel, n_tiles, seq // tm),
        grid=(n_tiles + 1,),
        in_specs=[
            pl.BlockSpec((tm, d), lambda i: (jnp.minimum(i, n_tiles - 1), 0)),
            _const_spec((bsz, N_MOD * d)),
            hbm_spec, hbm_spec, hbm_spec,
            _const_spec((1, d)),
            _const_spec((1, d)),
        ],
        out_specs=pl.BlockSpec((tm, d), lambda i: (jnp.maximum(i - 1, 0), 0)),
        out_shape=jax.ShapeDtypeStruct((t, d), F32),
        scratch_shapes=[
            pltpu.VMEM((tm, D_FF), BF16),
            pltpu.VMEM((tm, d), F32),
            pltpu.VMEM((nch, d, FF_CHUNK), BF16),
            pltpu.VMEM((nch, d, FF_CHUNK), BF16),
            pltpu.VMEM((D_FF, d), BF16),
            pltpu.SemaphoreType.DMA((3, nch)),
        ],
        compiler_params=pltpu.CompilerParams(
            dimension_semantics=("arbitrary",),
            vmem_limit_bytes=VMEM_LIMIT_BYTES),
        name="ffn_ln2",
    )(x1, mod, w_gate, w_up, w_down, ln_g, ln_b)


def _rotary_tables(seq):
    d = RET_HEAD_DIM
    inv_freq = ROPE_BASE ** (-np.arange(0, d, 2, dtype=np.float64) / d)
    ang = np.arange(seq, dtype=np.float64)[:, None] * inv_freq[None, :]
    cos, sin = np.cos(ang), np.sin(ang)
    cosq = np.concatenate([cos, cos], axis=-1)
    sinq = np.concatenate([-sin, sin], axis=-1)
    kscale = d ** -0.5
    return jnp.asarray(np.concatenate([cosq, sinq, cosq * kscale, sinq * kscale], axis=-1).astype(np.float32))


def _retention_consts():
    h, L, d = RET_HEADS, RET_CHUNK, RET_HEAD_DIM
    log_g = np.log1p(-(2.0 ** (-5.0 - np.arange(h, dtype=np.float64))))
    idx = np.arange(L, dtype=np.float64)
    diff = idx[:, None] - idx[None, :]
    dmask = np.where(diff >= 0, np.exp(log_g[:, None, None] * np.maximum(diff, 0.0)), 0.0)
    zeta = np.exp(log_g[:, None] * (L - 1 - idx)[None, :])
    xi = np.exp(log_g[:, None] * (idx + 1)[None, :])
    cdecay = np.exp(log_g * L)
    bcast = lambda a: np.broadcast_to(a[:, :, None], (h, L, d))
    return tuple(jnp.asarray(np.ascontiguousarray(t).astype(np.float32))
                 for t in (dmask, bcast(xi), bcast(zeta), cdecay))


def kernel(x, c, ada_w, ada_b, w_in, conv_w, w_out, ln1_g, ln1_b, w_gate, w_up, w_down, ln2_g, ln2_b):
    bsz, seq, d = x.shape
    assert ada_w.shape[0] == DEPTH
    assert seq % SEQ_TILE == 0 and seq % FFN_TILE == 0 and D_FF % FF_CHUNK == 0 and N_IN_COLS % IN_GROUP == 0
    assert SEQ_TILE % SUB_ROWS == 0 and FFN_TILE % FFN_SUB_ROWS == 0 and SUB_ROWS % RET_CHUNK == 0
    tables = _rotary_tables(seq)
    consts = _retention_consts()
    for layer in range(DEPTH):
        mod = _adaln(c, ada_w[layer], ada_b[layer])
        x1, wg_bf, wu_bf, wd_bf = _mixer(
            x, mod, w_in[layer], conv_w, w_out[layer],
            ln1_g[layer].reshape(1, d), ln1_b[layer].reshape(1, d), tables, consts,
            w_gate[layer], w_up[layer], w_down[layer])
        x2 = _ffn(x1.reshape(bsz * seq, d), mod, wg_bf, wu_bf, wd_bf,
                  ln2_g[layer].reshape(1, d), ln2_b[layer].reshape(1, d), seq)
        x = x2.reshape(bsz, seq, d)
    return x
```

```python
import functools

import jax
import jax.numpy as jnp
import numpy as np
from jax import lax
from jax.experimental import pallas as pl
from jax.experimental.pallas import tpu as pltpu

D_MODEL = 1024
CONV_WIDTH = 512
CONV_K = 3
RET_WIDTH = 512
RET_HEADS = 4
RET_HEAD_DIM = 128
RET_CHUNK = 128
ROPE_BASE = 10000.0
N_IN_COLS = 3 * CONV_WIDTH + 4 * RET_WIDTH
D_FF = 2816
DEPTH = 1
DN_ALPHA = float((2 * DEPTH) ** 0.25)
LN_EPS = 1e-5
N_MOD = 6

SEQ_TILE = 512
FFN_TILE = 512
SUB_ROWS = 256
FFN_SUB_ROWS = 256
FF_CHUNK = 256
IN_GROUP = 512
WAIT_STARTS = (0, 1, 3, 6)
HALO = 8
BF16_SUBLANES = 16
W_SLOTS = 3
VMEM_LIMIT_BYTES = 56 * 1024 * 1024

F32 = jnp.float32
BF16 = jnp.bfloat16


def _layernorm(v, g, b):
    mu = jnp.mean(v, axis=-1, keepdims=True)
    d = v - mu
    var = jnp.mean(d * d, axis=-1, keepdims=True)
    return d * lax.rsqrt(var + LN_EPS) * g + b


def _dot(a, b):
    return jnp.dot(a, b, preferred_element_type=F32)


def _wait_batches(n):
    starts = [k for k in WAIT_STARTS if k < n]
    return {k: e for k, e in zip(starts, starts[1:] + [n])}


def _const_spec(shape):
    zeros = (0,) * len(shape)
    return pl.BlockSpec(shape, lambda i: zeros, pipeline_mode=pl.Buffered(1))


def _mixer_kernel(n_tiles, tps, x_ref, c_ref, adaw_hbm, adab_ref, win_hbm, convw_ref, wout_hbm, g_ref, b_ref,
                  cosq_ref, sinq_ref, cosk_ref, sink_ref,
                  dmask_ref, xi_ref, zeta_ref, cdecay_ref,
                  wg_ref, wu_ref, wd_ref,
                  o_ref, mod_ref, wg_bf_ref, wu_bf_ref, wd_bf_ref,
                  z_scr, xres_scr, u_scr, state_scr, win_ref, wout_ref, stage_scr, wsem):
    for c in range(D_FF // FF_CHUNK):
        wg_bf_ref[c] = wg_ref[:, c * FF_CHUNK:(c + 1) * FF_CHUNK].astype(BF16)
        wu_bf_ref[c] = wu_ref[:, c * FF_CHUNK:(c + 1) * FF_CHUNK].astype(BF16)
    wd_bf_ref[...] = wd_ref[...].astype(BF16)

    ts = x_ref.shape[1]
    wc, wr, hd, L = CONV_WIDTH, RET_WIDTH, RET_HEAD_DIM, RET_CHUNK
    dm = D_MODEL
    gw = IN_GROUP
    n_groups = N_IN_COLS // gw
    i = pl.program_id(0)
    n_ada = N_MOD * dm // gw
    n_ada_first = 2 * dm // gw
    n_w = n_groups + dm // gw
    pieces = ([("a", k) for k in range(n_ada_first)] + [("w", p) for p in range(n_w)]
              + [("a", k) for k in range(n_ada_first, n_ada)])

    def piece_copy(q):
        kind, k = pieces[q]
        if kind == "a":
            src = adaw_hbm.at[:, pl.ds(k * gw, gw)]
        elif k < n_groups:
            src = win_hbm.at[:, pl.ds(k * gw, gw)]
        else:
            src = wout_hbm.at[:, pl.ds((k - n_groups) * gw, gw)]
        return pltpu.make_async_copy(src, stage_scr.at[q % W_SLOTS], wsem.at[q % W_SLOTS])

    def in_proj_stage(r0):
        b = jnp.minimum(i, n_tiles - 1) // tps
        rows = slice(r0, r0 + SUB_ROWS)
        x = x_ref[0, rows, :]
        h = (x * (1.0 + mod_ref[pl.ds(b, 1), dm:2 * dm]) + mod_ref[pl.ds(b, 1), 0:dm]).astype(BF16)

        def group(kk):
            z_scr[rows, kk * gw:(kk + 1) * gw] = _dot(h, win_ref[kk])

        def save_residual():
            xres_scr[rows, :] = DN_ALPHA * x

        return [functools.partial(group, kk) for kk in range(n_groups)], save_residual

    def mix_stage(r0, states):
        bp = jnp.maximum(i - 1, 0) // tps
        rows = slice(r0, r0 + SUB_ROWS)
        st = {}

        def load_and_rotate():
            u0 = HALO + r0
            u_scr[u0:u0 + SUB_ROWS, :] = z_scr[rows, 0:wc] * z_scr[rows, wc:2 * wc]
            conv = (convw_ref[0, 0:1, :] * u_scr[u0 - 2:u0 - 2 + SUB_ROWS, :]
                    + convw_ref[0, 1:2, :] * u_scr[u0 - 1:u0 - 1 + SUB_ROWS, :]
                    + convw_ref[0, 2:3, :] * u_scr[u0:u0 + SUB_ROWS, :])
            st["y_conv"] = (z_scr[rows, 2 * wc:3 * wc] * conv).astype(BF16)
            o = 3 * wc
            for c in range(SUB_ROWS // L):
                crow = slice(r0 + c * L, r0 + (c + 1) * L)
                cq, sq = cosq_ref[crow, :], sinq_ref[crow, :]
                ck, sk = cosk_ref[crow, :], sink_ref[crow, :]
                for hh in range(RET_HEADS):
                    qf = z_scr[crow, o + hh * hd:o + (hh + 1) * hd]
                    kf = z_scr[crow, o + wr + hh * hd:o + wr + (hh + 1) * hd]
                    vb = z_scr[crow, o + 2 * wr + hh * hd:o + 2 * wr + (hh + 1) * hd].astype(BF16)
                    qf = qf * cq + pltpu.roll(qf, hd // 2, 1) * sq
                    kf = kf * ck + pltpu.roll(kf, hd // 2, 1) * sk
                    st[c, hh] = dict(qb=qf.astype(BF16), kb=kf.astype(BF16),
                                     qx=(qf * xi_ref[hh]).astype(BF16),
                                     kz=(kf * zeta_ref[hh]).astype(BF16), vb=vb)

        def scores_and_kv():
            for c in range(SUB_ROWS // L):
                for hh in range(RET_HEADS):
                    e = st[c, hh]
                    e["s"] = lax.dot_general(e["qb"], e["kb"], (((1,), (1,)), ((), ())),
                                             preferred_element_type=F32)
                    e["kv"] = lax.dot_general(e["kz"], e["vb"], (((0,), (0,)), ((), ())),
                                              preferred_element_type=F32)

        def retention_out():
            r_rows = []
            for c in range(SUB_ROWS // L):
                r_heads = []
                for hh in range(RET_HEADS):
                    e = st[c, hh]
                    p = (e["s"] * dmask_ref[hh]).astype(BF16)
                    out = _dot(jnp.concatenate([p, e["qx"]], axis=1),
                               jnp.concatenate([e["vb"], states[hh].astype(BF16)], axis=0))
                    states[hh] = cdecay_ref[hh] * states[hh] + e["kv"]
                    mu = jnp.mean(out, axis=-1, keepdims=True)
                    dlt = out - mu
                    var = jnp.mean(dlt * dlt, axis=-1, keepdims=True)
                    r_heads.append(dlt * lax.rsqrt(var + LN_EPS))
                r_rows.append(jnp.concatenate(r_heads, axis=1))
            st["r"] = jnp.concatenate(r_rows, axis=0)

        def out_proj():
            gz = z_scr[rows, 3 * wc + 3 * wr:3 * wc + 4 * wr]
            y_ret = (jax.nn.silu(gz) * st["r"]).astype(BF16)
            st["mix"] = _dot(jnp.concatenate([st["y_conv"], y_ret], axis=1), wout_ref[...])

        def norm_store():
            gate = mod_ref[pl.ds(bp, 1), 2 * dm:3 * dm]
            o_ref[0, rows, :] = _layernorm(xres_scr[rows, :] + gate * st["mix"], g_ref[...], b_ref[...])

        return load_and_rotate, scores_and_kv, retention_out, out_proj, norm_store

    def mix_prologue():
        jp = jnp.maximum(i - 1, 0) % tps

        @pl.when(jp == 0)
        def _():
            state_scr[...] = jnp.zeros_like(state_scr)
            u_scr[0:HALO, :] = jnp.zeros((HALO, wc), F32)

        return [state_scr[hh] for hh in range(RET_HEADS)]

    def mix_epilogue(states):
        for hh in range(RET_HEADS):
            state_scr[hh] = states[hh]
        u_scr[0:HALO, :] = u_scr[ts:ts + HALO, :]

    @pl.when(i == 0)
    def _():
        sc = jax.nn.silu(c_ref[...]).astype(BF16)
        stages = None
        for q in range(min(W_SLOTS, len(pieces))):
            piece_copy(q).start()
        for q, (kind, k) in enumerate(pieces):
            piece_copy(q).wait()
            piece = stage_scr[q % W_SLOTS].astype(BF16)
            if kind == "a":
                mod_ref[:, k * gw:(k + 1) * gw] = _dot(sc, piece) + adab_ref[:, k * gw:(k + 1) * gw]
            elif k < n_groups:
                win_ref[k] = piece
            else:
                wout_ref[:, (k - n_groups) * gw:(k - n_groups + 1) * gw] = piece
            if q + W_SLOTS < len(pieces):
                piece_copy(q + W_SLOTS).start()
            if kind == "w" and k < n_groups:
                if stages is None:
                    stages = [in_proj_stage(r0) for r0 in range(0, ts, SUB_ROWS)]
                for groups, _ in stages:
                    groups[k]()
        for _, save_residual in stages:
            save_residual()

    @pl.when(i == n_tiles)
    def _():
        states = mix_prologue()
        for r0 in range(0, ts, SUB_ROWS):
            for stage in mix_stage(r0, states):
                stage()
        mix_epilogue(states)

    @pl.when((i > 0) & (i < n_tiles))
    def _():
        states = mix_prologue()
        for r0 in range(0, ts, SUB_ROWS):
            groups, save_residual = in_proj_stage(r0)
            load_and_rotate, scores_and_kv, retention_out, out_proj, norm_store = mix_stage(r0, states)
            load_and_rotate()
            groups[0]()
            groups[1]()
            scores_and_kv()
            groups[2]()
            groups[3]()
            retention_out()
            groups[4]()
            groups[5]()
            out_proj()
            groups[6]()
            norm_store()
            save_residual()
        mix_epilogue(states)


def _mixer(x, c, ada_w, ada_b, w_in, conv_w, w_out, ln_g, ln_b, tables, consts, w_gate, w_up, w_down):
    bsz, seq, d = x.shape
    ts = SEQ_TILE
    tps = seq // ts
    n_tiles = bsz * tps
    cosq, sinq, cosk, sink = tables
    dmask, xi_b, zeta_b, cdecay = consts
    hd = RET_HEAD_DIM
    cur = lambda i: jnp.minimum(i, n_tiles - 1)
    prev = lambda i: jnp.maximum(i - 1, 0)
    tab_spec = pl.BlockSpec((ts, hd), lambda i: (prev(i) % tps, 0))
    gu_rows = d // n_tiles
    dn_rows = D_FF // (n_tiles // 2)
    assert gu_rows * n_tiles == d and dn_rows * (n_tiles // 2) == D_FF
    assert gu_rows % BF16_SUBLANES == 0 and dn_rows % BF16_SUBLANES == 0
    nch = D_FF // FF_CHUNK
    gu_spec = pl.BlockSpec((gu_rows, D_FF), lambda i: (cur(i), 0))
    gu_out_spec = pl.BlockSpec((nch, gu_rows, FF_CHUNK), lambda i: (0, cur(i), 0))
    dn_spec = pl.BlockSpec((dn_rows, d), lambda i: (cur(i) // 2, 0))
    n_groups = N_IN_COLS // IN_GROUP
    hbm_spec = pl.BlockSpec(memory_space=pl.ANY)
    return pl.pallas_call(
        functools.partial(_mixer_kernel, n_tiles, tps),
        grid=(n_tiles + 1,),
        in_specs=[
            pl.BlockSpec((1, ts, d), lambda i: (cur(i) // tps, cur(i) % tps, 0)),
            _const_spec((bsz, d)),
            hbm_spec,
            _const_spec((1, N_MOD * d)),
            hbm_spec,
            _const_spec((1, CONV_K, CONV_WIDTH)),
            hbm_spec,
            _const_spec((1, d)),
            _const_spec((1, d)),
            tab_spec, tab_spec, tab_spec, tab_spec,
            _const_spec((RET_HEADS, RET_CHUNK, RET_CHUNK)),
            _const_spec((RET_HEADS, RET_CHUNK, hd)),
            _const_spec((RET_HEADS, RET_CHUNK, hd)),
            pl.BlockSpec(memory_space=pltpu.SMEM),
            gu_spec, gu_spec, dn_spec,
        ],
        out_specs=[
            pl.BlockSpec((1, ts, d), lambda i: (prev(i) // tps, prev(i) % tps, 0)),
            pl.BlockSpec((bsz, N_MOD * d), lambda i: (0, 0)),
            gu_out_spec, gu_out_spec, dn_spec,
        ],
        out_shape=[
            jax.ShapeDtypeStruct((bsz, seq, d), F32),
            jax.ShapeDtypeStruct((bsz, N_MOD * d), F32),
            jax.ShapeDtypeStruct((nch, d, FF_CHUNK), BF16),
            jax.ShapeDtypeStruct((nch, d, FF_CHUNK), BF16),
            jax.ShapeDtypeStruct((D_FF, d), BF16),
        ],
        scratch_shapes=[
            pltpu.VMEM((ts, N_IN_COLS), F32),
            pltpu.VMEM((ts, d), F32),
            pltpu.VMEM((ts + HALO, CONV_WIDTH), F32),
            pltpu.VMEM((RET_HEADS, hd, hd), F32),
            pltpu.VMEM((n_groups, d, IN_GROUP), BF16),
            pltpu.VMEM((d, d), BF16),
            pltpu.VMEM((W_SLOTS, d, IN_GROUP), F32),
            pltpu.SemaphoreType.DMA((W_SLOTS,)),
        ],
        compiler_params=pltpu.CompilerParams(
            dimension_semantics=("arbitrary",),
            vmem_limit_bytes=VMEM_LIMIT_BYTES),
        name="mixer_ln1",
    )(x, c, ada_w, ada_b.reshape(1, N_MOD * d), w_in, conv_w, w_out, ln_g, ln_b, cosq, sinq, cosk, sink,
      dmask, xi_b, zeta_b, cdecay, w_gate, w_up, w_down)


def _ffn_kernel(n_tiles, tiles_per_seq, x_ref, mod_ref, wg_hbm, wu_hbm, wd_hbm, g_ref, b_ref,
                o_ref, act_scr, xres_scr, wg_ref, wu_ref, wd_ref, wsem):
    tm = x_ref.shape[0]
    i = pl.program_id(0)
    dm = D_MODEL
    fc = FF_CHUNK
    nch = D_FF // fc
    batches = _wait_batches(nch)

    def gate_copy(c):
        return pltpu.make_async_copy(wg_hbm.at[c], wg_ref.at[c], wsem.at[0, c])

    def up_copy(c):
        return pltpu.make_async_copy(wu_hbm.at[c], wu_ref.at[c], wsem.at[1, c])

    def down_copy(c):
        rows = pl.ds(c * fc, fc)
        return pltpu.make_async_copy(wd_hbm.at[rows], wd_ref.at[rows], wsem.at[2, c])

    def down_and_norm():
        bp = jnp.maximum(i - 1, 0) // tiles_per_seq
        gate = mod_ref[pl.ds(bp, 1), 5 * dm:6 * dm]
        for r0 in range(0, tm, FFN_SUB_ROWS):
            rows = slice(r0, r0 + FFN_SUB_ROWS)
            ff = _dot(act_scr[rows, :], wd_ref[...])
            o_ref[rows, :] = _layernorm(xres_scr[rows, :] + gate * ff, g_ref[...], b_ref[...])

    def gate_up(wait_weights=False):
        b = jnp.minimum(i, n_tiles - 1) // tiles_per_seq
        shift = mod_ref[pl.ds(b, 1), 3 * dm:4 * dm]
        scale = mod_ref[pl.ds(b, 1), 4 * dm:5 * dm]
        for r0 in range(0, tm, FFN_SUB_ROWS):
            rows = slice(r0, r0 + FFN_SUB_ROWS)
            x = x_ref[rows, :]
            h = (x * (1.0 + scale) + shift).astype(BF16)
            for c in range(nch):
                if wait_weights and r0 == 0 and c in batches:
                    for cc in range(c, batches[c]):
                        gate_copy(cc).wait()
                        up_copy(cc).wait()
                act_scr[rows, c * fc:(c + 1) * fc] = (jax.nn.silu(_dot(h, wg_ref[c]))
                                                      * _dot(h, wu_ref[c])).astype(BF16)
            xres_scr[rows, :] = DN_ALPHA * x

    @pl.when(i == 0)
    def _():
        for c in range(nch):
            gate_copy(c).start()
            up_copy(c).start()
        for c in range(nch):
            down_copy(c).start()
        gate_up(wait_weights=True)
        for c in range(nch):
            down_copy(c).wait()

    @pl.when(i == n_tiles)
    def _():
        down_and_norm()

    @pl.when((i > 0) & (i < n_tiles))
    def _():
        down_and_norm()
        gate_up()


def _ffn(x1, mod, w_gate, w_up, w_down, ln_g, ln_b, seq):
    t, d = x1.shape
    bsz = t // seq
    tm = FFN_TILE
    n_tiles = t // tm
    nch = D_FF // FF_CHUNK
    hbm_spec = pl.BlockSpec(memory_space=pl.ANY)
    return pl.pallas_call(
        functools.partial(_ffn_kernel, n_tiles, seq // tm),
        grid=(n_tiles + 1,),
        in_specs=[
            pl.BlockSpec((tm, d), lambda i: (jnp.minimum(i, n_tiles - 1), 0)),
            _const_spec((bsz, N_MOD * d)),
            hbm_spec, hbm_spec, hbm_spec,
            _const_spec((1, d)),
            _const_spec((1, d)),
        ],
        out_specs=pl.BlockSpec((tm, d), lambda i: (jnp.maximum(i - 1, 0), 0)),
        out_shape=jax.ShapeDtypeStruct((t, d), F32),
        scratch_shapes=[
            pltpu.VMEM((tm, D_FF), BF16),
            pltpu.VMEM((tm, d), F32),
            pltpu.VMEM((nch, d, FF_CHUNK), BF16),
            pltpu.VMEM((nch, d, FF_CHUNK), BF16),
            pltpu.VMEM((D_FF, d), BF16),
            pltpu.SemaphoreType.DMA((3, nch)),
        ],
        compiler_params=pltpu.CompilerParams(
            dimension_semantics=("arbitrary",),
            vmem_limit_bytes=VMEM_LIMIT_BYTES),
        name="ffn_ln2",
    )(x1, mod, w_gate, w_up, w_down, ln_g, ln_b)


def _rotary_tables(seq):
    d = RET_HEAD_DIM
    inv_freq = ROPE_BASE ** (-np.arange(0, d, 2, dtype=np.float64) / d)
    ang = np.arange(seq, dtype=np.float64)[:, None] * inv_freq[None, :]
    cos, sin = np.cos(ang), np.sin(ang)
    cosq = np.concatenate([cos, cos], axis=-1)
    sinq = np.concatenate([-sin, sin], axis=-1)
    kscale = d ** -0.5
    return tuple(jnp.asarray(t.astype(np.float32)) for t in (cosq, sinq, cosq * kscale, sinq * kscale))


def _retention_consts():
    h, L, d = RET_HEADS, RET_CHUNK, RET_HEAD_DIM
    log_g = np.log1p(-(2.0 ** (-5.0 - np.arange(h, dtype=np.float64))))
    idx = np.arange(L, dtype=np.float64)
    diff = idx[:, None] - idx[None, :]
    dmask = np.where(diff >= 0, np.exp(log_g[:, None, None] * np.maximum(diff, 0.0)), 0.0)
    zeta = np.exp(log_g[:, None] * (L - 1 - idx)[None, :])
    xi = np.exp(log_g[:, None] * (idx + 1)[None, :])
    cdecay = np.exp(log_g * L)
    bcast = lambda a: np.broadcast_to(a[:, :, None], (h, L, d))
    return tuple(jnp.asarray(np.ascontiguousarray(t).astype(np.float32))
                 for t in (dmask, bcast(xi), bcast(zeta), cdecay))


def kernel(x, c, ada_w, ada_b, w_in, conv_w, w_out, ln1_g, ln1_b, w_gate, w_up, w_down, ln2_g, ln2_b):
    bsz, seq, d = x.shape
    assert ada_w.shape[0] == DEPTH
    assert seq % SEQ_TILE == 0 and seq % FFN_TILE == 0 and D_FF % FF_CHUNK == 0 and N_IN_COLS % IN_GROUP == 0
    assert SEQ_TILE % SUB_ROWS == 0 and FFN_TILE % FFN_SUB_ROWS == 0 and SUB_ROWS % RET_CHUNK == 0
    tables = _rotary_tables(seq)
    consts = _retention_consts()
    for layer in range(DEPTH):
        x1, mod, wg_bf, wu_bf, wd_bf = _mixer(
            x, c, ada_w[layer], ada_b[layer], w_in[layer], conv_w, w_out[layer],
            ln1_g[layer].reshape(1, d), ln1_b[layer].reshape(1, d), tables, consts,
            w_gate[layer], w_up[layer], w_down[layer])
        x2 = _ffn(x1.reshape(bsz * seq, d), mod, wg_bf, wu_bf, wd_bf,
                  ln2_g[layer].reshape(1, d), ln2_b[layer].reshape(1, d), seq)
        x = x2.reshape(bsz, seq, d)
    return x
```
